```python
import jax, jax.numpy as jnp
from jax import lax
import numpy as np

D_MODEL = 1024
BATCH = 4
SEQ = 4096
DEPTH = 4
DEC_BATCH = 32
DEC_SEQ = 4
PAST_LEN = 8192
PAGE_SIZE = 128

N_GROUPS = 3
LANES = 4
N_HEADS = N_GROUPS * LANES
HEAD_DIM = 64
ATT_W = N_HEADS * HEAD_DIM
LANE_W = LANES * HEAD_DIM
WINDOWS = (128, 512, 2048)
DILATIONS = (1, 4, 16)
Q_BLOCK = 128
CONV_CH = 512
CONV_WIDTH = 31
N_EXPERTS = 32
TOP_K = 4
D_FF = 1024
SWIGLU_LIMIT = 7.0
SWIGLU_ALPHA = 1.702
MOE_BLOCK = 128
N_IN = 3 * ATT_W + 2 * CONV_CH + 2 * D_MODEL
ALPHA_DN = (2 * DEPTH) ** 0.25
BETA_DN = (8 * DEPTH) ** -0.25
LN_EPS = 1e-5
NEG_INF = -1e30

kernel_name = "hybrid_dilated_conformer_moe_step"


def buffer_len(g):
    return min(WINDOWS[g], PAST_LEN)


def layer_norm(x, g, b):
    x32 = x.astype(jnp.float32)
    mu = jnp.mean(x32, -1, keepdims=True)
    var = jnp.mean(jnp.square(x32 - mu), -1, keepdims=True)
    return ((x32 - mu) * lax.rsqrt(var + LN_EPS) * g + b).astype(x.dtype)


def alibi_slopes():
    return jnp.exp2(-8.0 * (jnp.arange(N_HEADS, dtype=jnp.float32) + 1.0) / N_HEADS)


def dilated_attn_prompt(q, k, v, window, dil, slopes):
    B, S, Hg, E = q.shape
    n = window // dil
    L = S // dil
    blk = min(Q_BLOCK, L)
    nb = -(-L // blk)
    Lp = nb * blk
    fold = lambda t: t.reshape(B, L, dil, Hg, E)
    qf = jnp.pad(fold(q), ((0, 0), (0, Lp - L), (0, 0), (0, 0), (0, 0)))
    kf = jnp.pad(fold(k), ((0, 0), (n, Lp - L), (0, 0), (0, 0), (0, 0)))
    vf = jnp.pad(fold(v), ((0, 0), (n, Lp - L), (0, 0), (0, 0), (0, 0)))
    qb = qf.reshape(B, nb, blk, dil, Hg, E)
    base = jnp.arange(nb) * blk
    jj = jnp.arange(blk + n)
    idx = base[:, None] + jj[None, :]
    kb = kf[:, idx]
    vb = vf[:, idx]
    s = jnp.einsum('bnidhe,bnjdhe->bndhij', qb, kb, preferred_element_type=jnp.float32) * (HEAD_DIM ** -0.5)
    dist = jnp.arange(blk)[:, None] + n - jj[None, :]
    upos = base[:, None, None] + jj[None, None, :] - n
    valid = (dist >= 0) & (dist <= n) & (upos >= 0)
    bias = -(slopes * dil)[:, None, None] * dist.astype(jnp.float32)[None]
    s = jnp.where(valid[None, :, None, None], s + bias, NEG_INF)
    lse = jax.nn.logsumexp(s, axis=-1)
    p = jnp.exp(s - lse[..., None])
    o = jnp.einsum('bndhij,bnjdhe->bnidhe', p.astype(v.dtype), vb)
    o = o.reshape(B, Lp, dil, Hg, E)[:, :L].reshape(B, S, Hg, E)
    lse = lse.transpose(0, 1, 4, 2, 3).reshape(B, Lp, dil, Hg)[:, :L].reshape(B, S, Hg)
    return o, lse


def dilated_attn_sample(q, k_all, v_all, window, dil, slopes):
    T = q.shape[1]
    Lw = k_all.shape[1] - T
    n = window // dil
    j = jnp.arange(n + 1)
    idx = Lw + jnp.arange(T)[:, None] - dil * j[None, :]
    valid = idx >= 0
    idxc = jnp.maximum(idx, 0)
    kg = k_all[:, idxc]
    vg = v_all[:, idxc]
    s = jnp.einsum('bthe,btjhe->bhtj', q, kg, preferred_element_type=jnp.float32) * (HEAD_DIM ** -0.5)
    s = s - (slopes * dil)[:, None, None] * j.astype(jnp.float32)[None, None, :]
    s = jnp.where(valid[None, None], s, NEG_INF)
    lse = jax.nn.logsumexp(s, axis=-1)
    p = jnp.exp(s - lse[..., None])
    o = jnp.einsum('bhtj,btjhe->bthe', p.astype(v_all.dtype), vg)
    return o, lse.transpose(0, 2, 1)


def merge_groups(outs, lses, dtype):
    w = jax.nn.softmax(jnp.stack(lses, 0), axis=0)
    o = jnp.sum(w[..., None] * jnp.stack(outs, 0).astype(jnp.float32), axis=0)
    B, T = o.shape[:2]
    return o.reshape(B, T, LANE_W).astype(dtype)


def conv_module(u, prefix, conv_w, conv_b, ln_g, ln_b, w_pb, b_pb):
    full = jnp.concatenate([prefix.astype(u.dtype), u], axis=1)
    y = lax.conv_general_dilated(full, conv_w[:, None, :].astype(full.dtype), (1,), 'VALID',
                                 dimension_numbers=('NWC', 'WIO', 'NWC'),
                                 feature_group_count=CONV_CH) + conv_b
    y = jax.nn.silu(layer_norm(y, ln_g, ln_b))
    return y @ w_pb + b_pb, full[:, full.shape[1] - (CONV_WIDTH - 1):]


def token_mixer(h, bufs, conv_prefix, slopes, w_in, b_in, w_oa, b_oa, conv_w, conv_b,
                conv_ln_g, conv_ln_b, w_pb, b_pb, w_out, b_out):
    B, T, _ = h.shape
    z = h @ w_in + b_in
    q = z[..., :ATT_W].reshape(B, T, N_HEADS, HEAD_DIM)
    k = z[..., ATT_W:2 * ATT_W].reshape(B, T, N_HEADS, HEAD_DIM)
    v = z[..., 2 * ATT_W:3 * ATT_W].reshape(B, T, N_HEADS, HEAD_DIM)
    o1 = 3 * ATT_W
    glu = z[..., o1:o1 + 2 * CONV_CH]
    o2 = o1 + 2 * CONV_CH
    gate_a = jax.nn.sigmoid(z[..., o2:o2 + D_MODEL])
    gate_b = jax.nn.sigmoid(z[..., o2 + D_MODEL:])
    outs, lses, new_bufs = [], [], []
    for g in range(N_GROUPS):
        hs = slice(g * LANES, (g + 1) * LANES)
        qg, kg, vg = q[:, :, hs], k[:, :, hs], v[:, :, hs]
        sg = slopes[hs]
        kv_new = jnp.stack([kg, vg], axis=2)
        lw = buffer_len(g)
        if bufs is None:
            o, lse = dilated_attn_prompt(qg, kg, vg, WINDOWS[g], DILATIONS[g], sg)
            if T >= lw:
                tail = kv_new[:, T - lw:]
            else:
                tail = jnp.pad(kv_new, ((0, 0), (lw - T, 0), (0, 0), (0, 0), (0, 0)))
        else:
            full = jnp.concatenate([bufs[g].astype(kv_new.dtype), kv_new], axis=1)
            o, lse = dilated_attn_sample(qg, full[:, :, 0], full[:, :, 1], WINDOWS[g], DILATIONS[g], sg)
            tail = full[:, full.shape[1] - lw:]
        outs.append(o)
        lses.append(lse)
        new_bufs.append(tail)
    branch_a = merge_groups(outs, lses, h.dtype) @ w_oa + b_oa
    u = glu[..., :CONV_CH] * jax.nn.sigmoid(glu[..., CONV_CH:])
    branch_b, conv_tail = conv_module(u, conv_prefix, conv_w, conv_b, conv_ln_g, conv_ln_b, w_pb, b_pb)
    y = (gate_a * branch_a + gate_b * branch_b) @ w_out + b_out
    return y, new_bufs, conv_tail


def moe(h, w_r, b_r, w_up, b_up, w_down, b_down):
    N = h.shape[0]
    NK = N * TOP_K
    logits = (h @ w_r + b_r).astype(jnp.float32)
    top_v, top_i = lax.top_k(logits, TOP_K)
    gate = jax.nn.softmax(top_v, axis=-1)
    e_flat = top_i.reshape(-1)
    tok_flat = jnp.repeat(jnp.arange(N, dtype=jnp.int32), TOP_K)
    order = jnp.argsort(e_flat)
    se = e_flat[order]
    stok = tok_flat[order]
    sgate = gate.reshape(-1)[order]
    counts = jnp.bincount(e_flat, length=N_EXPERTS)
    padded = ((counts + MOE_BLOCK - 1) // MOE_BLOCK) * MOE_BLOCK
    start = jnp.cumsum(counts) - counts
    pend = jnp.cumsum(padded)
    pstart = pend - padded
    dest = pstart[se] + jnp.arange(NK) - start[se]
    nb = -(-NK // MOE_BLOCK) + N_EXPERTS
    cap = nb * MOE_BLOCK
    row_tok = jnp.full((cap,), N, dtype=jnp.int32).at[dest].set(stok)
    block_e = jnp.minimum(jnp.searchsorted(pend, jnp.arange(nb) * MOE_BLOCK, side='right'), N_EXPERTS - 1)
    hp = jnp.concatenate([h, jnp.zeros((1, h.shape[1]), h.dtype)], axis=0)
    xb = hp[row_tok].reshape(nb, MOE_BLOCK, h.shape[1])

    def expert_block(args):
        xblk, e = args
        up = xblk @ w_up[e] + b_up[e]
        a = jnp.minimum(up[:, :D_FF], SWIGLU_LIMIT)
        lin = jnp.clip(up[:, D_FF:], -SWIGLU_LIMIT, SWIGLU_LIMIT)
        act = a * jax.nn.sigmoid(SWIGLU_ALPHA * a) * (lin + 1.0)
        return act @ w_down[e] + b_down[e]

    yb = lax.map(expert_block, (xb, block_e))
    y_rows = yb.reshape(cap, -1)[dest]
    out = jax.ops.segment_sum(y_rows.astype(jnp.float32) * sgate[:, None], stok, num_segments=N)
    return out.astype(h.dtype)


def decoder_layer(x, c, bufs, conv_prefix, slopes, lw):
    (w_mod, b_mod, w_in, b_in, w_oa, b_oa, conv_w, conv_b, conv_ln_g, conv_ln_b, w_pb, b_pb,
     w_out, b_out, ln1_g, ln1_b, w_r, b_r, w_up, b_up, w_down, b_down, ln2_g, ln2_b) = lw
    mod = jax.nn.silu(c) @ w_mod + b_mod
    sh1, sc1, gt1, sh2, sc2, gt2 = jnp.split(mod[:, None, :], 6, axis=-1)
    h = x * (1.0 + sc1) + sh1
    y, new_bufs, conv_tail = token_mixer(h, bufs, conv_prefix, slopes, w_in, b_in, w_oa, b_oa,
                                         conv_w, conv_b, conv_ln_g, conv_ln_b, w_pb, b_pb, w_out, b_out)
    x = layer_norm(ALPHA_DN * x + (1.0 + gt1) * y, ln1_g, ln1_b)
    h = x * (1.0 + sc2) + sh2
    B, T, D = h.shape
    y = moe(h.reshape(B * T, D), w_r, b_r, w_up, b_up, w_down, b_down).reshape(B, T, D)
    x = layer_norm(ALPHA_DN * x + (1.0 + gt2) * y, ln2_g, ln2_b)
    return x, new_bufs, conv_tail


def setup_inputs(seed: int = 0) -> dict:
    key = jax.random.key(seed)
    ks = iter(jax.random.split(key, 40))
    nrm = lambda shape, scale: jax.random.normal(next(ks), shape, jnp.float32) * scale
    D = D_MODEL
    col_scale = jnp.ones((N_IN,), jnp.float32).at[2 * ATT_W:3 * ATT_W].set(BETA_DN)
    inp = {}
    inp['x_prompt'] = nrm((BATCH, SEQ, D), 1.0)
    inp['x_sample'] = nrm((DEC_BATCH, DEC_SEQ, D), 1.0)
    inp['cache_win0'] = nrm((DEPTH, DEC_BATCH, buffer_len(0), 2, LANES, HEAD_DIM), 1.0)
    inp['cache_win1'] = nrm((DEPTH, DEC_BATCH, buffer_len(1), 2, LANES, HEAD_DIM), 1.0)
    inp['cache_win2'] = nrm((DEPTH, DEC_BATCH, buffer_len(2), 2, LANES, HEAD_DIM), 1.0)
    inp['state_conv'] = nrm((DEPTH, DEC_BATCH, CONV_WIDTH - 1, CONV_CH), 0.5)
    inp['c_prompt'] = nrm((BATCH, D), 1.0)
    inp['c_sample'] = nrm((DEC_BATCH, D), 1.0)
    inp['w_mod'] = nrm((DEPTH, D, 6 * D), 0.1 * D ** -0.5)
    inp['b_mod'] = nrm((DEPTH, 6 * D), 0.01)
    inp['w_in'] = nrm((DEPTH, D, N_IN), D ** -0.5) * col_scale
    inp['b_in'] = nrm((DEPTH, N_IN), 0.01)
    inp['w_oa'] = nrm((DEPTH, LANE_W, D), LANE_W ** -0.5)
    inp['b_oa'] = nrm((DEPTH, D), 0.01)
    inp['conv_w'] = nrm((DEPTH, CONV_WIDTH, CONV_CH), CONV_WIDTH ** -0.5)
    inp['conv_b'] = nrm((DEPTH, CONV_CH), 0.01)
    inp['conv_ln_g'] = 1.0 + nrm((DEPTH, CONV_CH), 0.01)
    inp['conv_ln_b'] = nrm((DEPTH, CONV_CH), 0.01)
    inp['w_pb'] = nrm((DEPTH, CONV_CH, D), CONV_CH ** -0.5)
    inp['b_pb'] = nrm((DEPTH, D), 0.01)
    inp['w_out'] = nrm((DEPTH, D, D), BETA_DN * D ** -0.5)
    inp['b_out'] = nrm((DEPTH, D), 0.01)
    inp['ln1_g'] = 1.0 + nrm((DEPTH, D), 0.01)
    inp['ln1_b'] = nrm((DEPTH, D), 0.01)
    inp['w_router'] = nrm((DEPTH, D, N_EXPERTS), D ** -0.5)
    inp['b_router'] = nrm((DEPTH, N_EXPERTS), 0.01)
    inp['w_up'] = nrm((DEPTH, N_EXPERTS, D, 2 * D_FF), D ** -0.5)
    inp['b_up'] = nrm((DEPTH, N_EXPERTS, 2 * D_FF), 0.01)
    inp['w_down'] = nrm((DEPTH, N_EXPERTS, D_FF, D), BETA_DN * D_FF ** -0.5)
    inp['b_down'] = nrm((DEPTH, N_EXPERTS, D), 0.01)
    inp['ln2_g'] = 1.0 + nrm((DEPTH, D), 0.01)
    inp['ln2_b'] = nrm((DEPTH, D), 0.01)
    return inp


def reference(x_prompt, x_sample, cache_win0, cache_win1, cache_win2, state_conv, c_prompt, c_sample,
              w_mod, b_mod, w_in, b_in, w_oa, b_oa, conv_w, conv_b, conv_ln_g, conv_ln_b, w_pb, b_pb,
              w_out, b_out, ln1_g, ln1_b, w_router, b_router, w_up, b_up, w_down, b_down, ln2_g, ln2_b):
    slopes = alibi_slopes()
    xp, xs = x_prompt, x_sample
    wp = [[], [], []]
    ws = [[], [], []]
    cp, cs = [], []
    for l in range(DEPTH):
        lw = (w_mod[l], b_mod[l], w_in[l], b_in[l], w_oa[l], b_oa[l], conv_w[l], conv_b[l],
              conv_ln_g[l], conv_ln_b[l], w_pb[l], b_pb[l], w_out[l], b_out[l], ln1_g[l], ln1_b[l],
              w_router[l], b_router[l], w_up[l], b_up[l], w_down[l], b_down[l], ln2_g[l], ln2_b[l])
        zero_prefix = jnp.zeros((xp.shape[0], CONV_WIDTH - 1, CONV_CH), xp.dtype)
        xp, kvp, convp = decoder_layer(xp, c_prompt, None, zero_prefix, slopes, lw)
        xs, kvs, convs = decoder_layer(xs, c_sample, [cache_win0[l], cache_win1[l], cache_win2[l]],
                                       state_conv[l], slopes, lw)
        for g in range(N_GROUPS):
            wp[g].append(kvp[g])
            ws[g].append(kvs[g])
        cp.append(convp)
        cs.append(convs)
    win0_p, win1_p, win2_p = jnp.stack(wp[0]), jnp.stack(wp[1]), jnp.stack(wp[2])
    win0_s, win1_s, win2_s = jnp.stack(ws[0]), jnp.stack(ws[1]), jnp.stack(ws[2])
    conv_p = jnp.stack(cp)
    conv_s = jnp.stack(cs)
    return (xp, xs, win0_p, win1_p, win2_p, conv_p, win0_s, win1_s, win2_s, conv_s)
```

```python
import functools

import jax
import jax.numpy as jnp
from jax import lax
from jax.experimental import pallas as pl
from jax.experimental.pallas import tpu as pltpu
from jax.experimental.pallas import tpu_sc as plsc

F32 = jnp.float32
BF16 = jnp.bfloat16
U32 = jnp.uint32
I32 = jnp.int32

D_MODEL = 1024
N_GROUPS = 3
LANES = 4
HEAD_DIM = 64
LANE_W = LANES * HEAD_DIM
ATT_W = N_GROUPS * LANE_W
N_HEADS = N_GROUPS * LANES
WINDOWS = (128, 512, 2048)
DILATIONS = (1, 4, 16)
BAND = 128
CONV_CH = 512
CONV_WIDTH = 31
CONV_HALO = 32
N_EXPERTS = 32
TOP_K = 4
D_FF = 1024
SWIGLU_LIMIT = 7.0
SWIGLU_ALPHA = 1.702
N_IN = 3 * ATT_W + 2 * CONV_CH + 2 * D_MODEL
DEPTH_FOR_ALPHA = 4
ALPHA_DN = (2 * DEPTH_FOR_ALPHA) ** 0.25
LN_EPS = 1e-5
NEG_INF = -1e30

QBLK = 128
ROUTER_LANES = 128
MOE_TM = 256
PACK_W = D_MODEL // 2
ROW_LANE = 128
ROW_SUB = PACK_W // ROW_LANE
VMEM_LIMIT = 56 * 1024 * 1024

_NT = (((1,), (1,)), ((), ()))


def _cparams(*sem):
    return pltpu.CompilerParams(dimension_semantics=sem, vmem_limit_bytes=VMEM_LIMIT)


def _sigmoid(x):
    return 1.0 / (1.0 + jnp.exp(-x))


def _layer_norm(r, g, b):
    mu = jnp.mean(r, axis=-1, keepdims=True)
    d = r - mu
    var = jnp.mean(d * d, axis=-1, keepdims=True)
    return d * lax.rsqrt(var + LN_EPS) * g + b


def _pack_bf16_pair(lo, hi):
    lo_b = lax.bitcast_convert_type(lo.astype(BF16).astype(F32), U32)
    hi_b = lax.bitcast_convert_type(hi.astype(BF16).astype(F32), U32)
    return (lo_b >> 16) | (hi_b & jnp.uint32(0xFFFF0000))


def _unpack_bf16_pair(w):
    lo = lax.bitcast_convert_type(w << 16, F32)
    hi = lax.bitcast_convert_type(w & jnp.uint32(0xFFFF0000), F32)
    return lo, hi


def _store_row_words(ref, words):
    for j in range(ROW_SUB):
        ref[:, j, :] = words[:, j * ROW_LANE:(j + 1) * ROW_LANE]


def _load_row_words(ref):
    return jnp.concatenate([ref[:, j, :] for j in range(ROW_SUB)], axis=1)


def _mod_kernel(c_ref, w_ref, b_ref, o_ref):
    c = c_ref[...]
    a = (c * _sigmoid(c)).astype(BF16)
    o_ref[0] = jnp.dot(a, w_ref[0].astype(BF16), preferred_element_type=F32) + b_ref[0]


def adaln_mod(c_all, w_mod, b_mod):
    depth, d, n = w_mod.shape
    m = c_all.shape[0]
    tn = 1536
    return pl.pallas_call(
        _mod_kernel,
        grid=(depth, n // tn),
        in_specs=[pl.BlockSpec((m, d), lambda l, j: (0, 0)),
                  pl.BlockSpec((1, d, tn), lambda l, j: (l, 0, j)),
                  pl.BlockSpec((1, 1, tn), lambda l, j: (l, 0, j))],
        out_specs=pl.BlockSpec((1, m, tn), lambda l, j: (l, 0, j)),
        out_shape=jax.ShapeDtypeStruct((depth, m, n), F32),
        compiler_params=_cparams("arbitrary", "arbitrary"),
        name="adaln_mod",
    )(c_all, w_mod, b_mod.reshape(depth, 1, n))


def _in_proj_kernel(x_ref, mod_ref, w_ref, b_ref, q0, q1, q2, kv0, kv1, kv2, u_ref, ga_ref, gb_ref):
    x = x_ref[0]
    sh = mod_ref[0, :, 0:D_MODEL]
    sc = mod_ref[0, :, D_MODEL:2 * D_MODEL]
    h = (x * (1.0 + sc) + sh).astype(BF16)

    def proj(c0, c1):
        return jnp.dot(h, w_ref[:, c0:c1], preferred_element_type=F32) + b_ref[:, c0:c1]

    for g, (q_ref, kv_ref) in enumerate(((q0, kv0), (q1, kv1), (q2, kv2))):
        c = g * LANE_W
        q_ref[0] = (proj(c, c + LANE_W) * (HEAD_DIM ** -0.5)).astype(BF16)
        kv_ref[0, :, 0:LANE_W] = proj(ATT_W + c, ATT_W + c + LANE_W)
        kv_ref[0, :, LANE_W:2 * LANE_W] = proj(2 * ATT_W + c, 2 * ATT_W + c + LANE_W)
    o1 = 3 * ATT_W
    glu_a = proj(o1, o1 + CONV_CH)
    glu_b = proj(o1 + CONV_CH, o1 + 2 * CONV_CH)
    u_ref[0] = glu_a * _sigmoid(glu_b)
    o2 = o1 + 2 * CONV_CH
    ga_ref[0] = _sigmoid(proj(o2, o2 + D_MODEL)).astype(BF16)
    gb_ref[0] = _sigmoid(proj(o2 + D_MODEL, o2 + 2 * D_MODEL)).astype(BF16)


def in_proj(x, mod, w_bf, b, tm):
    g_, r, d = x.shape
    rm = mod.shape[1]
    mt = 1 if rm == 1 else tm
    row = lambda w: pl.BlockSpec((1, tm, w), lambda g, i: (g, i, 0))
    mod_spec = pl.BlockSpec((1, mt, 2 * d), (lambda g, i: (g, 0, 0)) if rm == 1 else (lambda g, i: (g, i, 0)))
    sds = lambda w, dt: jax.ShapeDtypeStruct((g_, r, w), dt)
    return pl.pallas_call(
        _in_proj_kernel,
        grid=(g_, r // tm),
        in_specs=[row(d), mod_spec,
                  pl.BlockSpec((d, N_IN), lambda g, i: (0, 0)),
                  pl.BlockSpec((1, N_IN), lambda g, i: (0, 0))],
        out_specs=[row(LANE_W)] * 3 + [row(2 * LANE_W)] * 3 + [row(CONV_CH), row(d), row(d)],
        out_shape=[sds(LANE_W, BF16)] * 3 + [sds(2 * LANE_W, F32)] * 3
                  + [sds(CONV_CH, F32), sds(d, BF16), sds(d, BF16)],
        compiler_params=_cparams("arbitrary", "arbitrary"),
        name="in_proj",
    )(x, mod, w_bf, b.reshape(1, N_IN))


def _attn_kernel(q_ref, kvp_ref, kvc_ref, ba_ref, bb_ref, o_ref, l_ref, *, qb):
    i = pl.program_id(2)
    for j in range(qb):
        rs = slice(j * QBLK, (j + 1) * QBLK)
        q = q_ref[0, rs, :]
        kva = kvp_ref[0] if j == 0 else kvc_ref[0, (j - 1) * QBLK:j * QBLK, :]
        kvb = kvc_ref[0, rs, :]
        outs, lses = [], []
        for h in range(LANES):
            ks = slice(h * HEAD_DIM, (h + 1) * HEAD_DIM)
            vs = slice(LANE_W + h * HEAD_DIM, LANE_W + (h + 1) * HEAD_DIM)
            qh = q[:, ks]
            sa = lax.dot_general(qh, kva[:, ks].astype(BF16), _NT, preferred_element_type=F32) + ba_ref[h]
            sb = lax.dot_general(qh, kvb[:, ks].astype(BF16), _NT, preferred_element_type=F32) + bb_ref[h]
            if j == 0:
                sa = jnp.where(i == 0, NEG_INF, sa)
            m = jnp.maximum(jnp.max(sa, axis=-1, keepdims=True), jnp.max(sb, axis=-1, keepdims=True))
            pa = jnp.exp(sa - m)
            pb = jnp.exp(sb - m)
            l = jnp.sum(pa, axis=-1, keepdims=True) + jnp.sum(pb, axis=-1, keepdims=True)
            o = (jnp.dot(pa.astype(BF16), kva[:, vs].astype(BF16), preferred_element_type=F32)
                 + jnp.dot(pb.astype(BF16), kvb[:, vs].astype(BF16), preferred_element_type=F32))
            outs.append(o / l)
            lses.append(jnp.broadcast_to(m + jnp.log(l), (QBLK, HEAD_DIM)))
        o_ref[0, rs, :] = jnp.concatenate(outs, axis=1).astype(BF16)
        l_ref[0, rs, :] = jnp.concatenate(lses, axis=1)


def _band_bias(slopes_g, dil):
    qi = jnp.arange(QBLK)[:, None]
    kj = jnp.arange(QBLK)[None, :]
    dist_a = qi + BAND - kj
    dist_b = qi - kj
    sl = -(slopes_g * dil)[:, None, None]
    ba = jnp.where((dist_a <= BAND)[None], sl * dist_a.astype(F32)[None], NEG_INF)
    bb = jnp.where((dist_b >= 0)[None], sl * dist_b.astype(F32)[None], NEG_INF)
    return ba.astype(F32), bb.astype(F32)


def attn_prompt(q, kv, slopes_g, dil):
    b_, s, _ = q.shape
    l_ = s // dil
    qb = 4 if l_ % (4 * QBLK) == 0 else (2 if l_ % (2 * QBLK) == 0 else 1)
    nb = l_ // (qb * QBLK)
    qf = q.reshape(b_, l_, dil * LANE_W)
    kvf = kv.reshape(b_, l_, dil * 2 * LANE_W)
    ba, bb = _band_bias(slopes_g, dil)
    bias_spec = pl.BlockSpec((LANES, QBLK, QBLK), lambda b, r, i: (0, 0, 0))
    o, lse = pl.pallas_call(
        functools.partial(_attn_kernel, qb=qb),
        grid=(b_, dil, nb),
        in_specs=[pl.BlockSpec((1, qb * QBLK, LANE_W), lambda b, r, i: (b, i, r)),
                  pl.BlockSpec((1, QBLK, 2 * LANE_W), lambda b, r, i: (b, jnp.maximum(i * qb - 1, 0), r)),
                  pl.BlockSpec((1, qb * QBLK, 2 * LANE_W), lambda b, r, i: (b, i, r)),
                  bias_spec, bias_spec],
        out_specs=[pl.BlockSpec((1, qb * QBLK, LANE_W), lambda b, r, i: (b, i, r))] * 2,
        out_shape=[jax.ShapeDtypeStruct((b_, l_, dil * LANE_W), BF16),
                   jax.ShapeDtypeStruct((b_, l_, dil * LANE_W), F32)],
        compiler_params=_cparams("arbitrary", "arbitrary", "arbitrary"),
        name="attn_prompt",
    )(qf, kvf, kvf, ba, bb)
    return o.reshape(b_, s, LANE_W), lse.reshape(b_, s, LANE_W)


SQ = 8
SKV = 16
SPAD = 128


def _attn_s_kernel(q_ref, kvn_ref, cache_ref, bias_ref, o_ref, l_ref, tail_ref, kf_ref, vf_ref, *, lw, t_new):
    nq = LANES * SQ
    rows = lax.broadcasted_iota(I32, (nq, LANE_W), 0)
    lanes = lax.broadcasted_iota(I32, (nq, LANE_W), 1)
    own = (rows >> 3) == (lanes >> 6)
    qrep = jnp.concatenate([q_ref[0]] * LANES, axis=0)
    qrows = jnp.where(own, qrep, jnp.zeros_like(qrep))
    kf_ref[0:lw, :] = cache_ref[0, 0, :, 0:LANE_W].astype(BF16)
    vf_ref[0:lw, :] = cache_ref[0, 0, :, LANE_W:2 * LANE_W].astype(BF16)
    kf_ref[lw:lw + SKV, :] = kvn_ref[0, :, 0:LANE_W].astype(BF16)
    vf_ref[lw:lw + SKV, :] = kvn_ref[0, :, LANE_W:2 * LANE_W].astype(BF16)
    zpad = jnp.zeros((SPAD - SKV, LANE_W), BF16)
    kf_ref[lw + SKV:lw + SPAD, :] = zpad
    vf_ref[lw + SKV:lw + SPAD, :] = zpad
    s = lax.dot_general(qrows, kf_ref[...], _NT, preferred_element_type=F32) + bias_ref[...]
    m = jnp.max(s, axis=-1, keepdims=True)
    p = jnp.exp(s - m)
    l = jnp.sum(p, axis=-1, keepdims=True)
    of = jnp.dot(p.astype(BF16), vf_ref[...], preferred_element_type=F32) / l
    lf = jnp.broadcast_to(m + jnp.log(l), (nq, LANE_W))
    of = jnp.where(own, of, 0.0)
    lf = jnp.where(own, lf, 0.0)
    o_ref[0] = (of[0:SQ] + of[SQ:2 * SQ] + of[2 * SQ:3 * SQ] + of[3 * SQ:4 * SQ]).astype(BF16)
    l_ref[0] = lf[0:SQ] + lf[SQ:2 * SQ] + lf[2 * SQ:3 * SQ] + lf[3 * SQ:4 * SQ]
    tail_ref[0, 0:lw - t_new, :] = cache_ref[0, 0, t_new:lw, :]
    tail_ref[0, lw - t_new:lw, :] = kvn_ref[0, 0:t_new, :]


def _sample_bias(slopes_g, dil, lw):
    t = jnp.arange(SQ)[:, None]
    r = jnp.arange(lw + SPAD)[None, :]
    dist = lw + t - r
    jn = dist // dil
    valid = (dist >= 0) & (dist % dil == 0) & (jn <= BAND) & (r < lw + SKV)
    sl = (slopes_g * dil)[:, None, None]
    bias = jnp.where(valid[None], -sl * jn.astype(F32)[None], NEG_INF)
    return bias.reshape(LANES * SQ, lw + SPAD).astype(F32)


def attn_sample(q, kvn, cache, layer, slopes_g, dil, t_new):
    db = q.shape[0]
    lw = cache.shape[2]
    bias = _sample_bias(slopes_g, dil, lw)
    return pl.pallas_call(
        functools.partial(_attn_s_kernel, lw=lw, t_new=t_new),
        grid=(db,),
        in_specs=[pl.BlockSpec((1, SQ, LANE_W), lambda b: (b, 0, 0)),
                  pl.BlockSpec((1, SKV, 2 * LANE_W), lambda b: (b, 0, 0)),
                  pl.BlockSpec((1, 1, lw, 2 * LANE_W), lambda b: (layer, b, 0, 0)),
                  pl.BlockSpec((LANES * SQ, lw + SPAD), lambda b: (0, 0))],
        out_specs=[pl.BlockSpec((1, SQ, LANE_W), lambda b: (b, 0, 0)),
                   pl.BlockSpec((1, SQ, LANE_W), lambda b: (b, 0, 0)),
                   pl.BlockSpec((1, lw, 2 * LANE_W), lambda b: (b, 0, 0))],
        out_shape=[jax.ShapeDtypeStruct((db, SQ, LANE_W), BF16),
                   jax.ShapeDtypeStruct((db, SQ, LANE_W), F32),
                   jax.ShapeDtypeStruct((db, lw, 2 * LANE_W), F32)],
        scratch_shapes=[pltpu.VMEM((lw + SPAD, LANE_W), BF16), pltpu.VMEM((lw + SPAD, LANE_W), BF16)],
        compiler_params=_cparams("arbitrary"),
        name="attn_sample",
    )(q, kvn, cache, bias)


CONV_ROWS = 32


def _conv_kernel(prev_ref, cur_ref, w_ref, cb_ref, g_ref, b_ref, o_ref, full_ref, *, zero_first):
    t_rows = cur_ref.shape[1]
    prev = prev_ref[0]
    if zero_first:
        prev = jnp.where(pl.program_id(1) == 0, 0.0, prev)
    full_ref[0:CONV_HALO, :] = prev
    full_ref[CONV_HALO:CONV_HALO + t_rows, :] = cur_ref[0]
    rows = min(CONV_ROWS, t_rows)
    off = CONV_HALO - (CONV_WIDTH - 1)
    for r0 in range(0, t_rows, rows):
        acc = jnp.zeros((rows, CONV_CH), F32)
        for w in range(CONV_WIDTH):
            acc = acc + full_ref[pl.ds(r0 + w + off, rows), :] * w_ref[w:w + 1, :]
        z = _layer_norm(acc + cb_ref[...], g_ref[...], b_ref[...])
        o_ref[0, r0:r0 + rows, :] = (z * _sigmoid(z)).astype(BF16)


def conv_branch(u, prev_src, conv_w, conv_b, ln_g, ln_b, tm, zero_first):
    g_, r, c = u.shape
    per = tm // CONV_HALO
    if zero_first:
        prev_map = lambda g, i: (g, jnp.maximum(i * per - 1, 0), 0)
    else:
        prev_map = lambda g, i: (g, 0, 0)
    vec = pl.BlockSpec((1, c), lambda g, i: (0, 0))
    return pl.pallas_call(
        functools.partial(_conv_kernel, zero_first=zero_first),
        grid=(g_, r // tm),
        in_specs=[pl.BlockSpec((1, CONV_HALO, c), prev_map),
                  pl.BlockSpec((1, tm, c), lambda g, i: (g, i, 0)),
                  pl.BlockSpec((CONV_WIDTH, c), lambda g, i: (0, 0)), vec, vec, vec],
        out_specs=pl.BlockSpec((1, tm, c), lambda g, i: (g, i, 0)),
        out_shape=jax.ShapeDtypeStruct((g_, r, c), BF16),
        scratch_shapes=[pltpu.VMEM((CONV_HALO + tm, c), F32)],
        compiler_params=_cparams("arbitrary", "arbitrary"),
        name="conv_branch",
    )(prev_src, u, conv_w, conv_b.reshape(1, c), ln_g.reshape(1, c), ln_b.reshape(1, c))


def _mix_kernel(o0, o1, o2, l0, l1, l2, cb_ref, ga_ref, gb_ref, x_ref, mod_ref,
                woa_ref, boa_ref, wpb_ref, bpb_ref, wout_ref, bout_ref, g1_ref, b1_ref,
                wrh_ref, wrl_ref, br_ref, x1_ref, hu_ref, lg_ref):
    la, lb, lc = l0[0], l1[0], l2[0]
    lmax = jnp.maximum(jnp.maximum(la, lb), lc)
    ea, eb, ec = jnp.exp(la - lmax), jnp.exp(lb - lmax), jnp.exp(lc - lmax)
    merged = (ea * o0[0].astype(F32) + eb * o1[0].astype(F32) + ec * o2[0].astype(F32)) / (ea + eb + ec)
    br_a = jnp.dot(merged.astype(BF16), woa_ref[...], preferred_element_type=F32) + boa_ref[...]
    br_b = jnp.dot(cb_ref[0], wpb_ref[...], preferred_element_type=F32) + bpb_ref[...]
    mixed = ga_ref[0].astype(F32) * br_a + gb_ref[0].astype(F32) * br_b
    y = jnp.dot(mixed.astype(BF16), wout_ref[...], preferred_element_type=F32) + bout_ref[...]
    gt1 = mod_ref[0, :, 2 * D_MODEL:3 * D_MODEL]
    sh2 = mod_ref[0, :, 3 * D_MODEL:4 * D_MODEL]
    sc2 = mod_ref[0, :, 4 * D_MODEL:5 * D_MODEL]
    x1 = _layer_norm(ALPHA_DN * x_ref[0] + (1.0 + gt1) * y, g1_ref[...], b1_ref[...])
    x1_ref[0] = x1
    h2 = x1 * (1.0 + sc2) + sh2
    hi = h2.astype(BF16)
    lo = (h2 - hi.astype(F32)).astype(BF16)
    lg_ref[0] = (jnp.dot(hi, wrh_ref[...], preferred_element_type=F32)
                 + jnp.dot(lo, wrh_ref[...], preferred_element_type=F32)
                 + jnp.dot(hi, wrl_ref[...], preferred_element_type=F32) + br_ref[...])
    _store_row_words(hu_ref.at[0], _pack_bf16_pair(h2[:, 0:PACK_W], h2[:, PACK_W:D_MODEL]))


def mix(o_g, l_g, cb, ga, gb, x, mod, wts, tm):
    g_, r, d = x.shape
    rm = mod.shape[1]
    mt = 1 if rm == 1 else tm
    row = lambda w: pl.BlockSpec((1, tm, w), lambda g, i: (g, i, 0))
    full = lambda a: pl.BlockSpec(a.shape, lambda g, i: (0,) * a.ndim)
    mod_spec = pl.BlockSpec((1, mt, 6 * d), (lambda g, i: (g, 0, 0)) if rm == 1 else (lambda g, i: (g, i, 0)))
    return pl.pallas_call(
        _mix_kernel,
        grid=(g_, r // tm),
        in_specs=[row(LANE_W)] * 6 + [row(CONV_CH), row(d), row(d), row(d), mod_spec] + [full(a) for a in wts],
        out_specs=[row(d), pl.BlockSpec((1, tm, ROW_SUB, ROW_LANE), lambda g, i: (g, i, 0, 0)), row(ROUTER_LANES)],
        out_shape=[jax.ShapeDtypeStruct((g_, r, d), F32),
                   jax.ShapeDtypeStruct((g_, r, ROW_SUB, ROW_LANE), U32),
                   jax.ShapeDtypeStruct((g_, r, ROUTER_LANES), F32)],
        compiler_params=_cparams("arbitrary", "arbitrary"),
        name="mix",
    )(*o_g, *l_g, cb, ga, gb, x, mod, *wts)


def _router_kernel(lg_ref, tri_ref, out_ref, cnt_ref, carry_ref):
    @pl.when(pl.program_id(0) == 0)
    def _():
        carry_ref[...] = jnp.zeros_like(carry_ref)

    work = lg_ref[...]
    tm = work.shape[0]
    lane = lax.broadcasted_iota(I32, (tm, ROUTER_LANES), 1)
    vals, idxs, hots = [], [], []
    for _ in range(TOP_K):
        m = jnp.max(work, axis=-1, keepdims=True)
        idx = jnp.min(jnp.where(work == m, lane, ROUTER_LANES), axis=-1, keepdims=True)
        hot = lane == idx
        work = jnp.where(hot, -jnp.inf, work)
        vals.append(m)
        idxs.append(idx)
        hots.append(hot)
    es = [jnp.exp(v - vals[0]) for v in vals]
    den = es[0] + es[1] + es[2] + es[3]
    member = jnp.where(hots[0] | hots[1] | hots[2] | hots[3], 1.0, 0.0)
    before = jnp.dot(tri_ref[...], member.astype(BF16), preferred_element_type=F32) + carry_ref[...]
    out = jnp.zeros((tm, ROUTER_LANES), F32)
    for k in range(TOP_K):
        rank = jnp.sum(jnp.where(hots[k], before, 0.0), axis=-1, keepdims=True)
        out = jnp.where(lane == k, es[k] / den, out)
        out = jnp.where(lane == TOP_K + k, idxs[k].astype(F32), out)
        out = jnp.where(lane == 2 * TOP_K + k, rank, out)
    out_ref[...] = out
    carry_ref[...] = carry_ref[...] + jnp.sum(member, axis=0, keepdims=True)
    cnt_ref[...] = carry_ref[...]


def router(logits):
    n = logits.shape[0]
    tm = next(t for t in (512, 384, 256, 128) if n % t == 0)
    tri = (jnp.arange(tm)[None, :] < jnp.arange(tm)[:, None]).astype(BF16)
    return pl.pallas_call(
        _router_kernel,
        grid=(n // tm,),
        in_specs=[pl.BlockSpec((tm, ROUTER_LANES), lambda i: (i, 0)),
                  pl.BlockSpec((tm, tm), lambda i: (0, 0))],
        out_specs=[pl.BlockSpec((tm, ROUTER_LANES), lambda i: (i, 0)),
                   pl.BlockSpec((1, ROUTER_LANES), lambda i: (0, 0))],
        out_shape=[jax.ShapeDtypeStruct((n, ROUTER_LANES), F32),
                   jax.ShapeDtypeStruct((1, ROUTER_LANES), F32)],
        scratch_shapes=[pltpu.VMEM((1, ROUTER_LANES), F32)],
        compiler_params=_cparams("arbitrary"),
        name="router",
    )(logits, tri)


def _moe_kernel(be_ref, nu_ref, xs_ref, wup_ref, bup_ref, wdn_ref, bdn_ref, y_ref, wup_bf, wdn_bf, act_ref):
    i = pl.program_id(0)

    @pl.when(i < nu_ref[0])
    def _():
        prev_e = be_ref[jnp.maximum(i - 1, 0)]

        @pl.when((i == 0) | (be_ref[i] != prev_e))
        def _():
            step = 128
            for r0 in range(0, D_MODEL, step):
                wup_bf[r0:r0 + step, :] = wup_ref[0, r0:r0 + step, :].astype(BF16)
            for r0 in range(0, D_FF, step):
                wdn_bf[r0:r0 + step, :] = wdn_ref[0, r0:r0 + step, :].astype(BF16)

        lo, hi = _unpack_bf16_pair(_load_row_words(xs_ref))
        x = jnp.concatenate([lo, hi], axis=1).astype(BF16)
        cw = 512
        for c0 in range(0, D_FF, cw):
            a = jnp.dot(x, wup_bf[:, c0:c0 + cw], preferred_element_type=F32) + bup_ref[0, :, c0:c0 + cw]
            lin = (jnp.dot(x, wup_bf[:, D_FF + c0:D_FF + c0 + cw], preferred_element_type=F32)
                   + bup_ref[0, :, D_FF + c0:D_FF + c0 + cw])
            a = jnp.minimum(a, SWIGLU_LIMIT)
            lin = jnp.clip(lin, -SWIGLU_LIMIT, SWIGLU_LIMIT)
            act_ref[:, c0:c0 + cw] = (a * _sigmoid(SWIGLU_ALPHA * a) * (lin + 1.0)).astype(BF16)
        y = jnp.dot(act_ref[...], wdn_bf[...], preferred_element_type=F32) + bdn_ref[0]
        _store_row_words(y_ref, _pack_bf16_pair(y[:, 0:PACK_W], y[:, PACK_W:D_MODEL]))


def moe_experts(xs, block_e, n_used, w_up, b_up, w_down, b_down):
    cap = xs.shape[0]
    nblk = cap // MOE_TM
    e = w_up.shape[0]
    row_map = lambda i, be, nu: (jnp.minimum(i, nu[0] - 1), 0, 0)
    exp_map = lambda i, be, nu: (be[i], 0, 0)
    grid_spec = pltpu.PrefetchScalarGridSpec(
        num_scalar_prefetch=2,
        grid=(nblk,),
        in_specs=[pl.BlockSpec((MOE_TM, ROW_SUB, ROW_LANE), row_map),
                  pl.BlockSpec((1, D_MODEL, 2 * D_FF), exp_map),
                  pl.BlockSpec((1, 1, 2 * D_FF), exp_map),
                  pl.BlockSpec((1, D_FF, D_MODEL), exp_map),
                  pl.BlockSpec((1, 1, D_MODEL), exp_map)],
        out_specs=pl.BlockSpec((MOE_TM, ROW_SUB, ROW_LANE), row_map),
        scratch_shapes=[pltpu.VMEM((D_MODEL, 2 * D_FF), BF16),
                        pltpu.VMEM((D_FF, D_MODEL), BF16),
                        pltpu.VMEM((MOE_TM, D_FF), BF16)],
    )
    return pl.pallas_call(
        _moe_kernel,
        grid_spec=grid_spec,
        out_shape=jax.ShapeDtypeStruct((cap, ROW_SUB, ROW_LANE), U32),
        compiler_params=_cparams("arbitrary"),
        name="moe_experts",
    )(block_e, n_used, xs, w_up, b_up.reshape(e, 1, 2 * D_FF), w_down, b_down.reshape(e, 1, D_MODEL))


def _final_kernel(y0, y1, y2, y3, r_ref, x_ref, mod_ref, g2_ref, b2_ref, o_ref):
    gates = r_ref[...]
    acc_lo = None
    for k, y_ref in enumerate((y0, y1, y2, y3)):
        lo, hi = _unpack_bf16_pair(_load_row_words(y_ref))
        gk = gates[:, k:k + 1]
        acc_lo = gk * lo if acc_lo is None else acc_lo + gk * lo
        acc_hi = gk * hi if k == 0 else acc_hi + gk * hi
    y = jnp.concatenate([acc_lo, acc_hi], axis=1)
    gt2 = mod_ref[0, :, 5 * D_MODEL:6 * D_MODEL]
    o_ref[0] = _layer_norm(ALPHA_DN * x_ref[0] + (1.0 + gt2) * y, g2_ref[...], b2_ref[...])


def final(yg, route, x1, mod, ln_g, ln_b, tm, row0):
    g_, r, d = x1.shape
    rm = mod.shape[1]
    mt = 1 if rm == 1 else tm
    per = r // tm
    base = row0 // tm
    tok = lambda g, i: base + g * per + i
    nblk_k = yg.shape[0] // (TOP_K * tm)
    yspec = lambda k: pl.BlockSpec((tm, ROW_SUB, ROW_LANE), lambda g, i: (k * nblk_k + tok(g, i), 0, 0))
    mod_spec = pl.BlockSpec((1, mt, 6 * d), (lambda g, i: (g, 0, 0)) if rm == 1 else (lambda g, i: (g, i, 0)))
    vec = pl.BlockSpec((1, d), lambda g, i: (0, 0))
    return pl.pallas_call(
        _final_kernel,
        grid=(g_, per),
        in_specs=[yspec(0), yspec(1), yspec(2), yspec(3),
                  pl.BlockSpec((tm, ROUTER_LANES), lambda g, i: (tok(g, i), 0)),
                  pl.BlockSpec((1, tm, d), lambda g, i: (g, i, 0)), mod_spec, vec, vec],
        out_specs=pl.BlockSpec((1, tm, d), lambda g, i: (g, i, 0)),
        out_shape=jax.ShapeDtypeStruct((g_, r, d), F32),
        compiler_params=_cparams("arbitrary", "arbitrary"),
        name="final",
    )(yg, yg, yg, yg, route, x1, mod, ln_g.reshape(1, d), ln_b.reshape(1, d))


SC_CORES = 2
SC_SUBCORES = 16
SC_WORKERS = SC_CORES * SC_SUBCORES
SC_CHUNK = 128


def _sc_mesh():
    return plsc.VectorSubcoreMesh(core_axis_name="c", subcore_axis_name="s")


def _sc_worker_chunks(nchunk):
    wid = lax.axis_index("s") * SC_CORES + lax.axis_index("c")
    return wid, (nchunk - wid + SC_WORKERS - 1) // SC_WORKERS


def dispatch_rows(hu, dest4, cap):
    n = hu.shape[0]
    nchunk = n // SC_CHUNK
    idx = dest4.reshape(TOP_K, nchunk, SC_CHUNK)

    @functools.partial(
        pl.kernel, mesh=_sc_mesh(),
        out_type=jax.ShapeDtypeStruct((cap, ROW_SUB, ROW_LANE), U32),
        scratch_types=[pltpu.VMEM((SC_CHUNK,), I32)] * TOP_K
                      + [pltpu.VMEM((SC_CHUNK, ROW_SUB, ROW_LANE), U32)])
    def k(x_hbm, i_hbm, o_hbm, i0, i1, i2, i3, rows_v):
        wid, n_mine = _sc_worker_chunks(nchunk)

        @pl.loop(0, n_mine)
        def _(j):
            c = wid + j * SC_WORKERS
            pltpu.sync_copy(x_hbm.at[pl.ds(c * SC_CHUNK, SC_CHUNK)], rows_v)
            for kk, iv in enumerate((i0, i1, i2, i3)):
                pltpu.sync_copy(i_hbm.at[kk, c], iv)
            for iv in (i0, i1, i2, i3):
                pltpu.sync_copy(rows_v, o_hbm.at[iv])

    return k(hu, idx)


def gather_rows(y, dest4):
    m = dest4.shape[0] * dest4.shape[1]
    nchunk = m // SC_CHUNK
    idx = dest4.reshape(nchunk, SC_CHUNK)

    @functools.partial(
        pl.kernel, mesh=_sc_mesh(),
        out_type=jax.ShapeDtypeStruct((m, ROW_SUB, ROW_LANE), U32),
        scratch_types=[pltpu.VMEM((SC_CHUNK,), I32), pltpu.VMEM((SC_CHUNK, ROW_SUB, ROW_LANE), U32)])
    def k(y_hbm, i_hbm, o_hbm, idx_v, rows_v):
        wid, n_mine = _sc_worker_chunks(nchunk)

        @pl.loop(0, n_mine)
        def _(j):
            c = wid + j * SC_WORKERS
            pltpu.sync_copy(i_hbm.at[c], idx_v)
            pltpu.sync_copy(y_hbm.at[idx_v], rows_v)
            pltpu.sync_copy(rows_v, o_hbm.at[pl.ds(c * SC_CHUNK, SC_CHUNK)])

    return k(y, idx)


def routing_plan(route, counts, n_tok):
    top_i = route[:, TOP_K:2 * TOP_K].astype(I32)
    rank = route[:, 2 * TOP_K:3 * TOP_K].astype(I32)
    cnt = counts[0, :N_EXPERTS].astype(I32)
    padded = ((cnt + MOE_TM - 1) // MOE_TM) * MOE_TM
    pend = jnp.cumsum(padded)
    pstart = pend - padded
    hot = top_i[:, :, None] == jnp.arange(N_EXPERTS, dtype=I32)[None, None, :]
    dest = rank + jnp.sum(jnp.where(hot, pstart[None, None, :], 0), axis=-1)
    nblk = -(-(n_tok * TOP_K) // MOE_TM) + N_EXPERTS
    n_used = pend[-1] // MOE_TM
    blk = jnp.arange(nblk, dtype=I32)
    blk_c = jnp.minimum(blk, n_used - 1)
    block_e = jnp.sum((blk_c[:, None] * MOE_TM >= pend[None, :]).astype(I32), axis=1)
    block_e = jnp.minimum(block_e, N_EXPERTS - 1)
    return dest.T, block_e, n_used.reshape(1).astype(I32), nblk * MOE_TM


def _alibi_slopes():
    return jnp.exp2(-8.0 * (jnp.arange(N_HEADS, dtype=F32) + 1.0) / N_HEADS)


def kernel(x_prompt, x_sample, cache_win0, cache_win1, cache_win2, state_conv, c_prompt, c_sample,
           w_mod, b_mod, w_in, b_in, w_oa, b_oa, conv_w, conv_b, conv_ln_g, conv_ln_b, w_pb, b_pb,
           w_out, b_out, ln1_g, ln1_b, w_router, b_router, w_up, b_up, w_down, b_down, ln2_g, ln2_b):
    depth = w_in.shape[0]
    b_, s, d = x_prompt.shape
    db, t_new, _ = x_sample.shape
    n_p, n_s = b_ * s, db * t_new
    n_tok = n_p + n_s
    tm_p = 512
    slopes = _alibi_slopes()
    caches = [c.reshape(c.shape[0], c.shape[1], c.shape[2], 2 * LANE_W)
              for c in (cache_win0, cache_win1, cache_win2)]

    mod_all = adaln_mod(jnp.concatenate([c_prompt, c_sample], axis=0), w_mod, b_mod)
    w_in_bf = w_in.astype(BF16)
    w_oa_bf, w_pb_bf, w_out_bf = w_oa.astype(BF16), w_pb.astype(BF16), w_out.astype(BF16)
    wr_pad = jnp.pad(w_router, ((0, 0), (0, 0), (0, ROUTER_LANES - N_EXPERTS)))
    wr_hi = wr_pad.astype(BF16)
    wr_lo = (wr_pad - wr_hi.astype(F32)).astype(BF16)
    br_pad = jnp.pad(b_router, ((0, 0), (0, ROUTER_LANES - N_EXPERTS)), constant_values=NEG_INF)

    xp = x_prompt
    xs_ = x_sample.reshape(1, n_s, d)
    wp = [[] for _ in range(N_GROUPS)]
    ws = [[] for _ in range(N_GROUPS)]
    cp, cs = [], []
    for l in range(depth):
        mod_p = mod_all[l, :b_].reshape(b_, 1, 6 * d)
        mod_s = jnp.repeat(mod_all[l, b_:], t_new, axis=0).reshape(1, n_s, 6 * d)

        pq0, pq1, pq2, pkv0, pkv1, pkv2, pu, pga, pgb = in_proj(xp, mod_p, w_in_bf[l], b_in[l], tm_p)
        pkv = (pkv0, pkv1, pkv2)
        po, pl_ = [], []
        for g, qg in enumerate((pq0, pq1, pq2)):
            o, lse = attn_prompt(qg, pkv[g], slopes[g * LANES:(g + 1) * LANES], DILATIONS[g])
            po.append(o)
            pl_.append(lse)
            wp[g].append(pkv[g][:, s - WINDOWS[g]:].reshape(b_, WINDOWS[g], 2, LANES, HEAD_DIM))
        pcb = conv_branch(pu, pu, conv_w[l], conv_b[l], conv_ln_g[l], conv_ln_b[l], tm_p, True)
        cp.append(pu[:, s - (CONV_WIDTH - 1):])

        sq0, sq1, sq2, skv0, skv1, skv2, su, sga, sgb = in_proj(xs_, mod_s, w_in_bf[l], b_in[l], n_s)
        skv = (skv0, skv1, skv2)
        so, sl_ = [], []
        for g, qg in enumerate((sq0, sq1, sq2)):
            q8 = jnp.pad(qg.reshape(db, t_new, LANE_W), ((0, 0), (0, SQ - t_new), (0, 0)))
            kv16 = jnp.pad(skv[g].reshape(db, t_new, 2 * LANE_W), ((0, 0), (0, SKV - t_new), (0, 0)))
            o, lse, tail = attn_sample(q8, kv16, caches[g], l, slopes[g * LANES:(g + 1) * LANES],
                                       DILATIONS[g], t_new)
            so.append(o[:, :t_new].reshape(1, n_s, LANE_W))
            sl_.append(lse[:, :t_new].reshape(1, n_s, LANE_W))
            ws[g].append(tail.reshape(db, WINDOWS[g], 2, LANES, HEAD_DIM))
        su3 = su.reshape(db, t_new, CONV_CH)
        full_s = jnp.concatenate([state_conv[l], su3], axis=1)
        cs.append(full_s[:, t_new:])
        st_pad = jnp.pad(state_conv[l], ((0, 0), (CONV_HALO - (CONV_WIDTH - 1), 0), (0, 0)))
        su8 = jnp.pad(su3, ((0, 0), (0, SQ - t_new), (0, 0)))
        scb = conv_branch(su8, st_pad, conv_w[l], conv_b[l], conv_ln_g[l], conv_ln_b[l], SQ, False)
        scb = scb[:, :t_new].reshape(1, n_s, CONV_CH)

        wts = (w_oa_bf[l], b_oa[l].reshape(1, d), w_pb_bf[l], b_pb[l].reshape(1, d),
               w_out_bf[l], b_out[l].reshape(1, d), ln1_g[l].reshape(1, d), ln1_b[l].reshape(1, d),
               wr_hi[l], wr_lo[l], br_pad[l].reshape(1, ROUTER_LANES))
        px1, phu, plg = mix(po, pl_, pcb, pga, pgb, xp, mod_p, wts, tm_p)
        sx1, shu, slg = mix(so, sl_, scb, sga, sgb, xs_, mod_s, wts, n_s)

        hu = jnp.concatenate([phu.reshape(n_p, ROW_SUB, ROW_LANE), shu.reshape(n_s, ROW_SUB, ROW_LANE)], axis=0)
        logits = jnp.concatenate([plg.reshape(n_p, ROUTER_LANES), slg.reshape(n_s, ROUTER_LANES)], axis=0)
        route, counts = router(logits)
        dest4, block_e, n_used, cap = routing_plan(route, counts, n_tok)
        xsort = dispatch_rows(hu, dest4, cap)
        ysort = moe_experts(xsort, block_e, n_used, w_up[l], b_up[l], w_down[l], b_down[l])
        n_pad = -(-n_tok // tm_p) * tm_p
        yg = gather_rows(ysort, jnp.pad(dest4, ((0, 0), (0, n_pad - n_tok))))

        xp = final(yg, route, px1, mod_p, ln2_g[l], ln2_b[l], tm_p, 0)
        xs_ = final(yg, route, sx1, mod_s, ln2_g[l], ln2_b[l], n_s, n_p)

    stack = lambda xs: jnp.stack(xs)
    return (xp, xs_.reshape(db, t_new, d),
            stack(wp[0]), stack(wp[1]), stack(wp[2]), stack(cp),
            stack(ws[0]), stack(ws[1]), stack(ws[2]), stack(cs))
```

```python
import functools

import jax
import jax.numpy as jnp
from jax import lax
from jax.experimental import pallas as pl
from jax.experimental.pallas import tpu as pltpu
from jax.experimental.pallas import tpu_sc as plsc

F32 = jnp.float32
BF16 = jnp.bfloat16
U32 = jnp.uint32
I32 = jnp.int32

D_MODEL = 1024
N_GROUPS = 3
LANES = 4
HEAD_DIM = 64
LANE_W = LANES * HEAD_DIM
ATT_W = N_GROUPS * LANE_W
N_HEADS = N_GROUPS * LANES
WINDOWS = (128, 512, 2048)
DILATIONS = (1, 4, 16)
BAND = 128
CONV_CH = 512
CONV_WIDTH = 31
CONV_HALO = 32
N_EXPERTS = 32
TOP_K = 4
D_FF = 1024
SWIGLU_LIMIT = 7.0
SWIGLU_ALPHA = 1.702
N_IN = 3 * ATT_W + 2 * CONV_CH + 2 * D_MODEL
DEPTH_FOR_ALPHA = 4
ALPHA_DN = (2 * DEPTH_FOR_ALPHA) ** 0.25
LN_EPS = 1e-5
NEG_INF = -1e30

VREG_LANES = 128
VREG_SUBLANES = 8
QBLK = 128
ROUTER_LANES = VREG_LANES
MOE_TM = 256
PACK_W = D_MODEL // 2
ROW_LANE = VREG_LANES
ROW_SUB = PACK_W // ROW_LANE
VMEM_LIMIT = 56 * 1024 * 1024

_NT = (((1,), (1,)), ((), ()))


def _cparams(*sem):
    return pltpu.CompilerParams(dimension_semantics=sem, vmem_limit_bytes=VMEM_LIMIT)


def _sigmoid(x):
    return 1.0 / (1.0 + jnp.exp(-x))


def _layer_norm(r, g, b):
    mu = jnp.mean(r, axis=-1, keepdims=True)
    d = r - mu
    var = jnp.mean(d * d, axis=-1, keepdims=True)
    return d * lax.rsqrt(var + LN_EPS) * g + b


def _pack_bf16_pair(lo, hi):
    lo_b = lax.bitcast_convert_type(lo.astype(BF16).astype(F32), U32)
    hi_b = lax.bitcast_convert_type(hi.astype(BF16).astype(F32), U32)
    return (lo_b >> 16) | (hi_b & jnp.uint32(0xFFFF0000))


def _pack_bf16_pair_native(lo, hi):
    return pltpu.pack_elementwise([lo, hi], packed_dtype=BF16)


def _unpack_bf16_pair(w):
    lo = lax.bitcast_convert_type(w << 16, F32)
    hi = lax.bitcast_convert_type(w & jnp.uint32(0xFFFF0000), F32)
    return lo, hi


def _store_row_words(ref, words):
    rows = words.shape[0]
    for j in range(ROW_SUB):
        ref[pl.ds(j, rows, stride=ROW_SUB), :] = words[:, j * ROW_LANE:(j + 1) * ROW_LANE]


def _load_row_words(ref):
    rows = ref.shape[0] // ROW_SUB
    return jnp.concatenate([ref[pl.ds(j, rows, stride=ROW_SUB), :] for j in range(ROW_SUB)], axis=1)


def _mod_spec(mod, tm, width, col_block):
    if mod.shape[1] == 1:
        return pl.BlockSpec((1, 1, width), lambda g, i: (g, 0, col_block))
    return pl.BlockSpec((1, tm, width), lambda g, i: (g, i, col_block))


def _layer_spec(a, layer):
    return pl.BlockSpec((1,) + a.shape[1:], lambda g, i: (layer,) + (0,) * (a.ndim - 1))


def _mod_kernel(c_ref, w_ref, b_ref, o_ref):
    c = c_ref[...]
    a = (c * _sigmoid(c)).astype(BF16)
    o_ref[0] = jnp.dot(a, w_ref[0].astype(BF16), preferred_element_type=F32) + b_ref[0]


def adaln_mod(c_all, w_mod, b_mod):
    depth, d, n = w_mod.shape
    m = c_all.shape[0]
    tn = 1536
    return pl.pallas_call(
        _mod_kernel,
        grid=(depth, n // tn),
        in_specs=[pl.BlockSpec((m, d), lambda l, j: (0, 0)),
                  pl.BlockSpec((1, d, tn), lambda l, j: (l, 0, j)),
                  pl.BlockSpec((1, 1, tn), lambda l, j: (l, 0, j))],
        out_specs=pl.BlockSpec((1, m, tn), lambda l, j: (l, 0, j)),
        out_shape=jax.ShapeDtypeStruct((depth, m, n), F32),
        compiler_params=_cparams("arbitrary", "arbitrary"),
        name="adaln_mod",
    )(c_all, w_mod, b_mod.reshape(depth, 1, n))


def _in_proj_kernel(x_ref, mod_ref, w_ref, b_ref, q0, q1, q2, kv0, kv1, kv2, u_ref, ga_ref, gb_ref,
                    zs_ref, *, dils):
    x = x_ref[0]
    tm = x.shape[0]
    sh = mod_ref[0, :, 0:D_MODEL]
    sc = mod_ref[0, :, D_MODEL:2 * D_MODEL]
    h = (x * (1.0 + sc) + sh).astype(BF16)

    def proj(c0, c1):
        return jnp.dot(h, w_ref[0, :, c0:c1], preferred_element_type=F32) + b_ref[0, :, c0:c1]

    def put(out_ref, col0, z, dil):
        if dil == 1:
            out_ref[0, :, col0:col0 + LANE_W] = z.astype(out_ref.dtype)
            return
        nt = LANE_W // VREG_LANES
        for c in range(nt):
            zs_ref[c] = z[:, c * VREG_LANES:(c + 1) * VREG_LANES]
        blk = out_ref.shape[2] // dil
        for r in range(dil):
            for c in range(nt):
                piece = zs_ref[c, pl.ds(r, tm // dil, stride=dil), :]
                lo = r * blk + col0 + c * VREG_LANES
                out_ref[0, :, lo:lo + VREG_LANES] = piece.astype(out_ref.dtype)

    for g, (q_ref, kv_ref) in enumerate(((q0, kv0), (q1, kv1), (q2, kv2))):
        c = g * LANE_W
        put(q_ref, 0, proj(c, c + LANE_W) * (HEAD_DIM ** -0.5), dils[g])
        put(kv_ref, 0, proj(ATT_W + c, ATT_W + c + LANE_W), dils[g])
        put(kv_ref, LANE_W, proj(2 * ATT_W + c, 2 * ATT_W + c + LANE_W), dils[g])
    o1 = 3 * ATT_W
    glu_a = proj(o1, o1 + CONV_CH)
    glu_b = proj(o1 + CONV_CH, o1 + 2 * CONV_CH)
    u_ref[0] = glu_a * _sigmoid(glu_b)
    o2 = o1 + 2 * CONV_CH
    ga_ref[0] = _sigmoid(proj(o2, o2 + D_MODEL)).astype(BF16)
    gb_ref[0] = _sigmoid(proj(o2 + D_MODEL, o2 + 2 * D_MODEL)).astype(BF16)


def in_proj(x, mod, w_bf, b, layer, tm, dils):
    g_, r, d = x.shape
    row = lambda w: pl.BlockSpec((1, tm, w), lambda g, i: (g, i, 0))
    fold = lambda w, dil: pl.BlockSpec((1, tm // dil, dil * w), lambda g, i: (g, i, 0))
    sds = lambda w, dt: jax.ShapeDtypeStruct((g_, r, w), dt)
    fsds = lambda w, dil, dt: jax.ShapeDtypeStruct((g_, r // dil, dil * w), dt)
    return pl.pallas_call(
        functools.partial(_in_proj_kernel, dils=dils),
        grid=(g_, r // tm),
        in_specs=[row(d), _mod_spec(mod, tm, 2 * d, 0), _layer_spec(w_bf, layer), _layer_spec(b, layer)],
        out_specs=[fold(LANE_W, dl) for dl in dils] + [fold(2 * LANE_W, dl) for dl in dils]
                  + [row(CONV_CH), row(d), row(d)],
        out_shape=[fsds(LANE_W, dl, BF16) for dl in dils] + [fsds(2 * LANE_W, dl, F32) for dl in dils]
                  + [sds(CONV_CH, F32), sds(d, BF16), sds(d, BF16)],
        scratch_shapes=[pltpu.VMEM((LANE_W // VREG_LANES, tm, VREG_LANES), F32)],
        compiler_params=_cparams("arbitrary", "arbitrary"),
        name="in_proj",
    )(x, mod, w_bf, b)


def _attn_kernel(q_ref, kvp_ref, kvc_ref, ba_ref, bb_ref, o_ref, l_ref, *, qb):
    i = pl.program_id(2)
    head_of_lane = lax.broadcasted_iota(I32, (QBLK, LANE_W), 1) >> 6
    ones = jnp.ones((QBLK, VREG_LANES), BF16)
    for j in range(qb):
        rs = slice(j * QBLK, (j + 1) * QBLK)
        q = q_ref[0, rs, :]
        q4 = jnp.concatenate([jnp.where(head_of_lane == h, q, jnp.zeros_like(q)) for h in range(LANES)], axis=0)
        if j == 0:
            ka, va = kvp_ref[0, :, 0:LANE_W].astype(BF16), kvp_ref[0, :, LANE_W:2 * LANE_W].astype(BF16)
        else:
            prs = slice((j - 1) * QBLK, j * QBLK)
            ka, va = kvc_ref[0, prs, 0:LANE_W].astype(BF16), kvc_ref[0, prs, LANE_W:2 * LANE_W].astype(BF16)
        kb, vb = kvc_ref[0, rs, 0:LANE_W].astype(BF16), kvc_ref[0, rs, LANE_W:2 * LANE_W].astype(BF16)
        sa = lax.dot_general(q4, ka, _NT, preferred_element_type=F32) + ba_ref[...]
        sb = lax.dot_general(q4, kb, _NT, preferred_element_type=F32) + bb_ref[...]
        if j == 0:
            sa = jnp.where(i == 0, NEG_INF, sa)
        m = jnp.max(jnp.maximum(sa, sb), axis=-1, keepdims=True)
        pa = jnp.exp(sa - m).astype(BF16)
        pb = jnp.exp(sb - m).astype(BF16)
        l = (jnp.dot(pa, ones, preferred_element_type=F32)
             + jnp.dot(pb, ones, preferred_element_type=F32))
        o4 = (jnp.dot(pa, va, preferred_element_type=F32)
              + jnp.dot(pb, vb, preferred_element_type=F32))
        inv = 1.0 / l
        o4 = o4 * jnp.concatenate([inv, inv], axis=1)
        lse = m + jnp.log(l)
        lse2 = jnp.concatenate([lse, lse], axis=1)
        o, ls = o4[0:QBLK], lse2[0:QBLK]
        for h in range(1, LANES):
            sel = head_of_lane == h
            o = jnp.where(sel, o4[h * QBLK:(h + 1) * QBLK], o)
            ls = jnp.where(sel, lse2[h * QBLK:(h + 1) * QBLK], ls)
        o_ref[0, rs, :] = o.astype(BF16)
        l_ref[0, rs, :] = ls


def _band_bias(slopes_g, dil):
    qi = jnp.arange(QBLK)[:, None]
    kj = jnp.arange(QBLK)[None, :]
    dist_a = qi + BAND - kj
    dist_b = qi - kj
    sl = -(slopes_g * dil)[:, None, None]
    ba = jnp.where((dist_a <= BAND)[None], sl * dist_a.astype(F32)[None], NEG_INF)
    bb = jnp.where((dist_b >= 0)[None], sl * dist_b.astype(F32)[None], NEG_INF)
    return ba.astype(F32).reshape(LANES * QBLK, QBLK), bb.astype(F32).reshape(LANES * QBLK, QBLK)


def attn_prompt(qf, kvf, slopes_g, dil):
    b_, l_, _ = qf.shape
    qb = 4 if l_ % (4 * QBLK) == 0 else (2 if l_ % (2 * QBLK) == 0 else 1)
    nb = l_ // (qb * QBLK)
    ba, bb = _band_bias(slopes_g, dil)
    bias_spec = pl.BlockSpec((LANES * QBLK, QBLK), lambda b, r, i: (0, 0))
    return pl.pallas_call(
        functools.partial(_attn_kernel, qb=qb),
        grid=(b_, dil, nb),
        in_specs=[pl.BlockSpec((1, qb * QBLK, LANE_W), lambda b, r, i: (b, i, r)),
                  pl.BlockSpec((1, QBLK, 2 * LANE_W), lambda b, r, i: (b, jnp.maximum(i * qb - 1, 0), r)),
                  pl.BlockSpec((1, qb * QBLK, 2 * LANE_W), lambda b, r, i: (b, i, r)),
                  bias_spec, bias_spec],
        out_specs=[pl.BlockSpec((1, qb * QBLK, LANE_W), lambda b, r, i: (b, i, r))] * 2,
        out_shape=[jax.ShapeDtypeStruct((b_, l_, dil * LANE_W), BF16),
                   jax.ShapeDtypeStruct((b_, l_, dil * LANE_W), F32)],
        compiler_params=_cparams("arbitrary", "arbitrary", "arbitrary"),
        name="attn_prompt",
    )(qf, kvf, kvf, ba, bb)


SQ = 8
SPAD = 128


def _attn_s_kernel(x_ref, mod_ref, wq_ref, bq_ref, wkv_ref, bkv_ref, cache_ref, bias_ref, _tails_in,
                   o_ref, l_ref, tail_ref, hpad_ref, *, lw, t_new):
    nq = LANES * SQ
    blk = 2 * SQ

    @pl.when(pl.program_id(0) == 0)
    def _():
        hpad_ref[0:SPAD - blk, :] = jnp.zeros((SPAD - blk, D_MODEL), BF16)

    sh = mod_ref[0, :, 0:D_MODEL]
    sc = mod_ref[0, :, D_MODEL:2 * D_MODEL]
    h8 = x_ref[0] * (1.0 + sc) + sh
    h16 = jnp.concatenate([jnp.zeros_like(h8), h8], axis=0).astype(BF16)
    hpad_ref[SPAD - blk:SPAD, :] = h16
    q16 = (jnp.dot(h16, wq_ref[0], preferred_element_type=F32) + bq_ref[0]) * (HEAD_DIM ** -0.5)
    q8 = q16[SQ:blk].astype(BF16)
    head_of_lane = lax.broadcasted_iota(I32, (SQ, LANE_W), 1) >> 6
    q4 = jnp.concatenate([jnp.where(head_of_lane == h, q8, jnp.zeros_like(q8)) for h in range(LANES)], axis=0)
    kvn = lax.dot_general(wkv_ref[0, 0], hpad_ref[...], _NT, preferred_element_type=F32) + bkv_ref[0, 0]
    kc = cache_ref[0, 0, 0:LANE_W, :].astype(BF16)
    vc = cache_ref[0, 0, LANE_W:2 * LANE_W, :].astype(BF16)
    s_c = jnp.dot(q4, kc, preferred_element_type=F32) + bias_ref[:, 0:lw]
    s_n = jnp.dot(q4, kvn[0:LANE_W].astype(BF16), preferred_element_type=F32) + bias_ref[:, lw:lw + SPAD]
    m = jnp.maximum(jnp.max(s_c, axis=-1, keepdims=True), jnp.max(s_n, axis=-1, keepdims=True))
    p_c = jnp.exp(s_c - m)
    p_n = jnp.exp(s_n - m)
    l = jnp.sum(p_c, axis=-1, keepdims=True) + jnp.sum(p_n, axis=-1, keepdims=True)
    o4 = (lax.dot_general(p_c.astype(BF16), vc, _NT, preferred_element_type=F32)
          + lax.dot_general(p_n.astype(BF16), kvn[LANE_W:2 * LANE_W].astype(BF16), _NT,
                            preferred_element_type=F32)) / l
    lf = jnp.broadcast_to(m + jnp.log(l), (nq, LANE_W))
    rows = lax.broadcasted_iota(I32, (nq, LANE_W), 0)
    lanes = lax.broadcasted_iota(I32, (nq, LANE_W), 1)
    own = (rows >> 3) == (lanes >> 6)
    o4 = jnp.where(own, o4, 0.0)
    lf = jnp.where(own, lf, 0.0)
    o_ref[0] = (o4[0:SQ] + o4[SQ:2 * SQ] + o4[2 * SQ:3 * SQ] + o4[3 * SQ:4 * SQ]).astype(BF16)
    l_ref[0] = lf[0:SQ] + lf[SQ:2 * SQ] + lf[2 * SQ:3 * SQ] + lf[3 * SQ:4 * SQ]
    rolled = pltpu.roll(cache_ref[0, 0], lw - t_new, 1)
    if lw > SPAD:
        tail_ref[0, 0, :, 0:lw - SPAD] = rolled[:, 0:lw - SPAD]
    lane = lax.broadcasted_iota(I32, (2 * LANE_W, SPAD), 1)
    tail_ref[0, 0, :, lw - SPAD:lw] = jnp.where(lane >= SPAD - t_new, kvn, rolled[:, lw - SPAD:lw])


def _sample_bias(slopes_g, dil, lw, t_new):
    t = jnp.maximum(jnp.arange(SQ) - (SQ - t_new), 0)[:, None]
    r = jnp.arange(lw + SPAD)[None, :]
    pos = jnp.where(r < lw, r, r - (SPAD - t_new))
    dist = lw + t - pos
    jn = dist // dil
    real = (r < lw) | (r >= lw + SPAD - t_new)
    valid = real & (dist >= 0) & (dist % dil == 0) & (jn <= BAND)
    sl = (slopes_g * dil)[:, None, None]
    bias = jnp.where(valid[None], -sl * jn.astype(F32)[None], NEG_INF)
    return bias.reshape(LANES * SQ, lw + SPAD).astype(F32)


def attn_sample(x8, mod, w_bf, b3, wkv_t, bkv_t, cache_t, tails, layer, g, slopes_g, dil, t_new):
    db = x8.shape[0]
    depth, _, ch, lw = cache_t.shape
    bias = _sample_bias(slopes_g, dil, lw, t_new)
    in_specs = [pl.BlockSpec((1, SQ, D_MODEL), lambda b: (b, 0, 0)),
                pl.BlockSpec((1, 1, 2 * D_MODEL), lambda b: (b, 0, 0)),
                pl.BlockSpec((1, D_MODEL, LANE_W), lambda b: (layer, 0, g)),
                pl.BlockSpec((1, 1, LANE_W), lambda b: (layer, 0, g)),
                pl.BlockSpec((1, 1, ch, D_MODEL), lambda b: (layer, g, 0, 0)),
                pl.BlockSpec((1, 1, ch, SPAD), lambda b: (layer, g, 0, 0)),
                pl.BlockSpec((1, 1, ch, lw), lambda b: (layer, b, 0, 0)),
                pl.BlockSpec((LANES * SQ, lw + SPAD), lambda b: (0, 0)),
                pl.BlockSpec(memory_space=pl.ANY)]
    args = [x8, mod, w_bf, b3, wkv_t, bkv_t, cache_t, bias, tails]
    aliases = {len(args) - 1: 2}
    return pl.pallas_call(
        functools.partial(_attn_s_kernel, lw=lw, t_new=t_new),
        grid=(db,),
        in_specs=in_specs,
        out_specs=[pl.BlockSpec((1, SQ, LANE_W), lambda b: (b, 0, 0)),
                   pl.BlockSpec((1, SQ, LANE_W), lambda b: (b, 0, 0)),
                   pl.BlockSpec((1, 1, ch, lw), lambda b: (layer, b, 0, 0))],
        out_shape=[jax.ShapeDtypeStruct((db, SQ, LANE_W), BF16),
                   jax.ShapeDtypeStruct((db, SQ, LANE_W), F32),
                   jax.ShapeDtypeStruct((depth, db, ch, lw), F32)],
        scratch_shapes=[pltpu.VMEM((SPAD, D_MODEL), BF16)],
        input_output_aliases=aliases,
        compiler_params=_cparams("arbitrary"),
        name="attn_sample",
    )(*args)


CONV_ROWS = 32


def _conv_kernel(prev_ref, cur_ref, w_ref, cb_ref, g_ref, b_ref, o_ref, sh_ref, *, zero_first):
    t_rows = cur_ref.shape[1]
    n = CONV_HALO + t_rows
    prev = prev_ref[0]
    if zero_first:
        prev = jnp.where(pl.program_id(1) == 0, 0.0, prev)
    sh_ref[0, 0:CONV_HALO, :] = prev
    sh_ref[0, CONV_HALO:n, :] = cur_ref[0]
    for s in range(1, VREG_SUBLANES):
        sh_ref[s, 0:n - VREG_SUBLANES, :] = sh_ref[0, pl.ds(s, n - VREG_SUBLANES), :]
    rows = min(CONV_ROWS, t_rows)
    off = CONV_HALO - (CONV_WIDTH - 1)
    ngrp = rows // VREG_SUBLANES
    for r0 in range(0, t_rows, rows):
        accs = [None] * ngrp
        for w in range(CONV_WIDTH):
            s = (w + off) % VREG_SUBLANES
            a = r0 + (w + off) - s
            w8 = w_ref[0, w * VREG_SUBLANES:(w + 1) * VREG_SUBLANES, :]
            for k in range(ngrp):
                lo = a + k * VREG_SUBLANES
                term = sh_ref[s, lo:lo + VREG_SUBLANES, :] * w8
                accs[k] = term if accs[k] is None else accs[k] + term
        acc = jnp.concatenate(accs, axis=0) if ngrp > 1 else accs[0]
        z = _layer_norm(acc + cb_ref[0], g_ref[0], b_ref[0])
        o_ref[0, r0:r0 + rows, :] = (z * _sigmoid(z)).astype(BF16)


def conv_branch(u, prev_src, conv_w, conv_b, ln_g, ln_b, layer, tm, zero_first):
    g_, r, c = u.shape
    per = tm // CONV_HALO
    if zero_first:
        prev_map = lambda g, i: (g, jnp.maximum(i * per - 1, 0), 0)
    else:
        prev_map = lambda g, i: (g, 0, 0)
    return pl.pallas_call(
        functools.partial(_conv_kernel, zero_first=zero_first),
        grid=(g_, r // tm),
        in_specs=[pl.BlockSpec((1, CONV_HALO, c), prev_map),
                  pl.BlockSpec((1, tm, c), lambda g, i: (g, i, 0)),
                  _layer_spec(conv_w, layer), _layer_spec(conv_b, layer),
                  _layer_spec(ln_g, layer), _layer_spec(ln_b, layer)],
        out_specs=pl.BlockSpec((1, tm, c), lambda g, i: (g, i, 0)),
        out_shape=jax.ShapeDtypeStruct((g_, r, c), BF16),
        scratch_shapes=[pltpu.VMEM((VREG_SUBLANES, CONV_HALO + tm, c), F32)],
        compiler_params=_cparams("arbitrary", "arbitrary"),
        name="conv_branch",
    )(prev_src, u, conv_w, conv_b, ln_g, ln_b)


def _mix_kernel(o0, o1, o2, l0, l1, l2, cb_ref, ga_ref, gb_ref, x_ref, mod_ref,
                woa_ref, boa_ref, wpb_ref, bpb_ref, wout_ref, bout_ref, g1_ref, b1_ref,
                wrh_ref, wrl_ref, br_ref, x1_ref, hu_ref, lg_ref, s0, s1, s2, s3, *, dils):
    tm = x_ref.shape[1]

    def unfold(ref, dil, scratch):
        if dil == 1:
            return ref[0].astype(F32)
        nt = LANE_W // VREG_LANES
        for r in range(dil):
            for c in range(nt):
                lo = r * LANE_W + c * VREG_LANES
                scratch[c, pl.ds(r, tm // dil, stride=dil), :] = ref[0, :, lo:lo + VREG_LANES].astype(F32)
        return jnp.concatenate([scratch[c] for c in range(nt)], axis=1)

    assert dils[0] == 1
    oa, ob, oc = unfold(o0, 1, None), unfold(o1, dils[1], s0), unfold(o2, dils[2], s1)
    la, lb, lc = unfold(l0, 1, None), unfold(l1, dils[1], s2), unfold(l2, dils[2], s3)
    lmax = jnp.maximum(jnp.maximum(la, lb), lc)
    ea, eb, ec = jnp.exp(la - lmax), jnp.exp(lb - lmax), jnp.exp(lc - lmax)
    merged = (ea * oa + eb * ob + ec * oc) / (ea + eb + ec)
    br_a = jnp.dot(merged.astype(BF16), woa_ref[0], preferred_element_type=F32) + boa_ref[0]
    br_b = jnp.dot(cb_ref[0], wpb_ref[0], preferred_element_type=F32) + bpb_ref[0]
    mixed = ga_ref[0].astype(F32) * br_a + gb_ref[0].astype(F32) * br_b
    y = jnp.dot(mixed.astype(BF16), wout_ref[0], preferred_element_type=F32) + bout_ref[0]
    gt1 = mod_ref[0, :, 2 * D_MODEL:3 * D_MODEL]
    sh2 = mod_ref[0, :, 3 * D_MODEL:4 * D_MODEL]
    sc2 = mod_ref[0, :, 4 * D_MODEL:5 * D_MODEL]
    x1 = _layer_norm(ALPHA_DN * x_ref[0] + (1.0 + gt1) * y, g1_ref[0], b1_ref[0])
    x1_ref[0] = x1
    h2 = x1 * (1.0 + sc2) + sh2
    hi = h2.astype(BF16)
    lo = (h2 - hi.astype(F32)).astype(BF16)
    lg_ref[0] = (jnp.dot(hi, wrh_ref[0], preferred_element_type=F32)
                 + jnp.dot(lo, wrh_ref[0], preferred_element_type=F32)
                 + jnp.dot(hi, wrl_ref[0], preferred_element_type=F32) + br_ref[0])
    _store_row_words(hu_ref.at[0], _pack_bf16_pair(h2[:, 0:PACK_W], h2[:, PACK_W:D_MODEL]))


def mix(o_g, l_g, cb, ga, gb, x, mod, wts, layer, tm, dils):
    g_, r, d = x.shape
    row = lambda w: pl.BlockSpec((1, tm, w), lambda g, i: (g, i, 0))
    fold = lambda dil: pl.BlockSpec((1, tm // dil, dil * LANE_W), lambda g, i: (g, i, 0))
    return pl.pallas_call(
        functools.partial(_mix_kernel, dils=dils),
        grid=(g_, r // tm),
        in_specs=[fold(dl) for dl in dils] * 2 + [row(CONV_CH), row(d), row(d), row(d), _mod_spec(mod, tm, 6 * d, 0)]
                 + [_layer_spec(a, layer) for a in wts],
        out_specs=[row(d), pl.BlockSpec((1, tm * ROW_SUB, ROW_LANE), lambda g, i: (g, i, 0)), row(ROUTER_LANES)],
        out_shape=[jax.ShapeDtypeStruct((g_, r, d), F32),
                   jax.ShapeDtypeStruct((g_, r * ROW_SUB, ROW_LANE), U32),
                   jax.ShapeDtypeStruct((g_, r, ROUTER_LANES), F32)],
        scratch_shapes=[pltpu.VMEM((LANE_W // VREG_LANES, tm, VREG_LANES), F32)] * 4,
        compiler_params=_cparams("arbitrary", "arbitrary"),
        name="mix",
    )(*o_g, *l_g, cb, ga, gb, x, mod, *wts)


def _router_kernel(lg_ref, tri_ref, out_ref, cnt_ref, carry_ref):
    @pl.when(pl.program_id(0) == 0)
    def _():
        carry_ref[...] = jnp.zeros_like(carry_ref)

    work = lg_ref[...]
    tm = work.shape[0]
    lane = lax.broadcasted_iota(I32, (tm, ROUTER_LANES), 1)
    vals, idxs, hots = [], [], []
    for _ in range(TOP_K):
        m = jnp.max(work, axis=-1, keepdims=True)
        idx = jnp.min(jnp.where(work == m, lane, ROUTER_LANES), axis=-1, keepdims=True)
        hot = lane == idx
        work = jnp.where(hot, -jnp.inf, work)
        vals.append(m)
        idxs.append(idx)
        hots.append(hot)
    es = [jnp.exp(v - vals[0]) for v in vals]
    den = es[0] + es[1] + es[2] + es[3]
    member = jnp.where(hots[0] | hots[1] | hots[2] | hots[3], 1.0, 0.0)
    before = jnp.dot(tri_ref[...], member.astype(BF16), preferred_element_type=F32) + carry_ref[...]
    out = jnp.zeros((tm, ROUTER_LANES), F32)
    for k in range(TOP_K):
        rank = jnp.sum(jnp.where(hots[k], before, 0.0), axis=-1, keepdims=True)
        out = jnp.where(lane == k, es[k] / den, out)
        out = jnp.where(lane == TOP_K + k, idxs[k].astype(F32), out)
        out = jnp.where(lane == 2 * TOP_K + k, rank, out)
    out_ref[...] = out
    carry_ref[...] = carry_ref[...] + jnp.sum(member, axis=0, keepdims=True)
    cnt_ref[...] = carry_ref[...]


def router(logits):
    n = logits.shape[0]
    tm = next(t for t in (512, 384, 256, 128) if n % t == 0)
    tri = (jnp.arange(tm)[None, :] < jnp.arange(tm)[:, None]).astype(BF16)
    return pl.pallas_call(
        _router_kernel,
        grid=(n // tm,),
        in_specs=[pl.BlockSpec((tm, ROUTER_LANES), lambda i: (i, 0)),
                  pl.BlockSpec((tm, tm), lambda i: (0, 0))],
        out_specs=[pl.BlockSpec((tm, ROUTER_LANES), lambda i: (i, 0)),
                   pl.BlockSpec((1, ROUTER_LANES), lambda i: (0, 0))],
        out_shape=[jax.ShapeDtypeStruct((n, ROUTER_LANES), F32),
                   jax.ShapeDtypeStruct((1, ROUTER_LANES), F32)],
        scratch_shapes=[pltpu.VMEM((1, ROUTER_LANES), F32)],
        compiler_params=_cparams("arbitrary"),
        name="router",
    )(logits, tri)


def _moe_kernel(be_ref, nu_ref, xs_ref, wup_ref, bup_ref, wdn_ref, bdn_ref, y_ref, wup_bf, wdn_bf, act_ref):
    i = pl.program_id(0)

    @pl.when(i < nu_ref[0])
    def _():
        prev_e = be_ref[jnp.maximum(i - 1, 0)]

        @pl.when((i == 0) | (be_ref[i] != prev_e))
        def _():
            step = 128
            for r0 in range(0, D_MODEL, step):
                wup_bf[r0:r0 + step, :] = wup_ref[0, 0, r0:r0 + step, :].astype(BF16)
            for r0 in range(0, D_FF, step):
                wdn_bf[r0:r0 + step, :] = wdn_ref[0, 0, r0:r0 + step, :].astype(BF16)

        lo, hi = _unpack_bf16_pair(_load_row_words(xs_ref))
        x = jnp.concatenate([lo, hi], axis=1).astype(BF16)
        cw = 512
        for c0 in range(0, D_FF, cw):
            a = jnp.dot(x, wup_bf[:, c0:c0 + cw], preferred_element_type=F32) + bup_ref[0, 0, :, c0:c0 + cw]
            lin = (jnp.dot(x, wup_bf[:, D_FF + c0:D_FF + c0 + cw], preferred_element_type=F32)
                   + bup_ref[0, 0, :, D_FF + c0:D_FF + c0 + cw])
            a = jnp.minimum(a, SWIGLU_LIMIT)
            lin = jnp.clip(lin, -SWIGLU_LIMIT, SWIGLU_LIMIT)
            act_ref[:, c0:c0 + cw] = (a * _sigmoid(SWIGLU_ALPHA * a) * (lin + 1.0)).astype(BF16)
        y = jnp.dot(act_ref[...], wdn_bf[...], preferred_element_type=F32) + bdn_ref[0, 0]
        _store_row_words(y_ref, _pack_bf16_pair_native(y[:, 0:PACK_W], y[:, PACK_W:D_MODEL]))


def moe_experts(xs, block_e, n_used, w_up, b_up, w_down, b_down, layer):
    cap = xs.shape[0] // ROW_SUB
    nblk = cap // MOE_TM
    row_map = lambda i, be, nu: (jnp.minimum(i, nu[0] - 1), 0)
    exp_map = lambda i, be, nu: (layer, be[i], 0, 0)
    grid_spec = pltpu.PrefetchScalarGridSpec(
        num_scalar_prefetch=2,
        grid=(nblk,),
        in_specs=[pl.BlockSpec((MOE_TM * ROW_SUB, ROW_LANE), row_map),
                  pl.BlockSpec((1, 1, D_MODEL, 2 * D_FF), exp_map),
                  pl.BlockSpec((1, 1, 1, 2 * D_FF), exp_map),
                  pl.BlockSpec((1, 1, D_FF, D_MODEL), exp_map),
                  pl.BlockSpec((1, 1, 1, D_MODEL), exp_map)],
        out_specs=pl.BlockSpec((MOE_TM * ROW_SUB, ROW_LANE), row_map),
        scratch_shapes=[pltpu.VMEM((D_MODEL, 2 * D_FF), BF16),
                        pltpu.VMEM((D_FF, D_MODEL), BF16),
                        pltpu.VMEM((MOE_TM, D_FF), BF16)],
    )
    return pl.pallas_call(
        _moe_kernel,
        grid_spec=grid_spec,
        out_shape=jax.ShapeDtypeStruct((cap * ROW_SUB, ROW_LANE), U32),
        compiler_params=_cparams("arbitrary"),
        name="moe_experts",
    )(block_e, n_used, xs, w_up, b_up, w_down, b_down)


def _final_kernel(y0, y1, y2, y3, r_ref, x_ref, mod_ref, g2_ref, b2_ref, o_ref):
    gates = r_ref[...]
    acc_lo = None
    for k, y_ref in enumerate((y0, y1, y2, y3)):
        lo, hi = _unpack_bf16_pair(_load_row_words(y_ref))
        gk = gates[:, k:k + 1]
        acc_lo = gk * lo if acc_lo is None else acc_lo + gk * lo
        acc_hi = gk * hi if k == 0 else acc_hi + gk * hi
    y = jnp.concatenate([acc_lo, acc_hi], axis=1)
    gt2 = mod_ref[0]
    o_ref[0] = _layer_norm(ALPHA_DN * x_ref[0] + (1.0 + gt2) * y, g2_ref[0], b2_ref[0])


def final(yg, route, x1, mod, ln_g, ln_b, layer, tm, row0):
    g_, r, d = x1.shape
    per = r // tm
    base = row0 // tm
    tok = lambda g, i: base + g * per + i
    nblk_k = yg.shape[0] // (TOP_K * tm * ROW_SUB)
    yspec = lambda k: pl.BlockSpec((tm * ROW_SUB, ROW_LANE), lambda g, i: (k * nblk_k + tok(g, i), 0))
    return pl.pallas_call(
        _final_kernel,
        grid=(g_, per),
        in_specs=[yspec(0), yspec(1), yspec(2), yspec(3),
                  pl.BlockSpec((tm, ROUTER_LANES), lambda g, i: (tok(g, i), 0)),
                  pl.BlockSpec((1, tm, d), lambda g, i: (g, i, 0)), _mod_spec(mod, tm, d, 5),
                  _layer_spec(ln_g, layer), _layer_spec(ln_b, layer)],
        out_specs=pl.BlockSpec((1, tm, d), lambda g, i: (g, i, 0)),
        out_shape=jax.ShapeDtypeStruct((g_, r, d), F32),
        compiler_params=_cparams("arbitrary", "arbitrary"),
        name="final",
    )(yg, yg, yg, yg, route, x1, mod, ln_g, ln_b)


SC_CORES = 2
SC_SUBCORES = 16
SC_WORKERS = SC_CORES * SC_SUBCORES
SC_CHUNK = 128


def _sc_mesh():
    return plsc.VectorSubcoreMesh(core_axis_name="c", subcore_axis_name="s")


def _sc_worker_chunks(nchunk):
    wid = lax.axis_index("s") * SC_CORES + lax.axis_index("c")
    return wid, (nchunk - wid + SC_WORKERS - 1) // SC_WORKERS


def dispatch_rows(hu, dest4, cap):
    n = hu.shape[0]
    nchunk = n // SC_CHUNK
    idx = dest4.reshape(TOP_K, nchunk, SC_CHUNK)

    @functools.partial(
        pl.kernel, mesh=_sc_mesh(),
        out_type=jax.ShapeDtypeStruct((cap, ROW_SUB, ROW_LANE), U32),
        scratch_types=[pltpu.VMEM((SC_CHUNK,), I32)] * TOP_K
                      + [pltpu.VMEM((SC_CHUNK, ROW_SUB, ROW_LANE), U32)])
    def k(x_hbm, i_hbm, o_hbm, i0, i1, i2, i3, rows_v):
        wid, n_mine = _sc_worker_chunks(nchunk)

        @pl.loop(0, n_mine)
        def _(j):
            c = wid + j * SC_WORKERS
            pltpu.sync_copy(x_hbm.at[pl.ds(c * SC_CHUNK, SC_CHUNK)], rows_v)
            for kk, iv in enumerate((i0, i1, i2, i3)):
                pltpu.sync_copy(i_hbm.at[kk, c], iv)
            for iv in (i0, i1, i2, i3):
                pltpu.sync_copy(rows_v, o_hbm.at[iv])

    return k(hu, idx)


def gather_rows(y, dest4):
    m = dest4.shape[0] * dest4.shape[1]
    nchunk = m // SC_CHUNK
    idx = dest4.reshape(nchunk, SC_CHUNK)

    @functools.partial(
        pl.kernel, mesh=_sc_mesh(),
        out_type=jax.ShapeDtypeStruct((m, ROW_SUB, ROW_LANE), U32),
        scratch_types=[pltpu.VMEM((SC_CHUNK,), I32), pltpu.VMEM((SC_CHUNK, ROW_SUB, ROW_LANE), U32)])
    def k(y_hbm, i_hbm, o_hbm, idx_v, rows_v):
        wid, n_mine = _sc_worker_chunks(nchunk)

        @pl.loop(0, n_mine)
        def _(j):
            c = wid + j * SC_WORKERS
            pltpu.sync_copy(i_hbm.at[c], idx_v)
            pltpu.sync_copy(y_hbm.at[idx_v], rows_v)
            pltpu.sync_copy(rows_v, o_hbm.at[pl.ds(c * SC_CHUNK, SC_CHUNK)])

    return k(y, idx)


def routing_plan(route, counts, n_tok):
    top_i = route[:, TOP_K:2 * TOP_K].astype(I32)
    rank = route[:, 2 * TOP_K:3 * TOP_K].astype(I32)
    cnt = counts[0, :N_EXPERTS].astype(I32)
    padded = ((cnt + MOE_TM - 1) // MOE_TM) * MOE_TM
    pend = jnp.cumsum(padded)
    pstart = pend - padded
    hot = top_i[:, :, None] == jnp.arange(N_EXPERTS, dtype=I32)[None, None, :]
    dest = rank + jnp.sum(jnp.where(hot, pstart[None, None, :], 0), axis=-1)
    nblk = -(-(n_tok * TOP_K) // MOE_TM) + N_EXPERTS
    n_used = pend[-1] // MOE_TM
    blk = jnp.arange(nblk, dtype=I32)
    blk_c = jnp.minimum(blk, n_used - 1)
    block_e = jnp.sum((blk_c[:, None] * MOE_TM >= pend[None, :]).astype(I32), axis=1)
    block_e = jnp.minimum(block_e, N_EXPERTS - 1)
    return dest.T, block_e, n_used.reshape(1).astype(I32), nblk * MOE_TM


def _alibi_slopes():
    return jnp.exp2(-8.0 * (jnp.arange(N_HEADS, dtype=F32) + 1.0) / N_HEADS)


def _row3(a):
    return a.reshape(a.shape[0], 1, a.shape[1])


def kernel(x_prompt, x_sample, cache_win0, cache_win1, cache_win2, state_conv, c_prompt, c_sample,
           w_mod, b_mod, w_in, b_in, w_oa, b_oa, conv_w, conv_b, conv_ln_g, conv_ln_b, w_pb, b_pb,
           w_out, b_out, ln1_g, ln1_b, w_router, b_router, w_up, b_up, w_down, b_down, ln2_g, ln2_b):
    depth = w_in.shape[0]
    b_, s, d = x_prompt.shape
    db, t_new, _ = x_sample.shape
    n_p, n_s = b_ * s, db * t_new
    n_tok = n_p + n_s
    tm_p = 512
    no_fold = (1,) * N_GROUPS
    slopes = _alibi_slopes()
    caches = [jnp.transpose(c, (0, 1, 3, 4, 5, 2)).reshape(c.shape[0], c.shape[1], 2 * LANE_W, c.shape[2])
              for c in (cache_win0, cache_win1, cache_win2)]
    wkv = w_in[:, :, ATT_W:3 * ATT_W].reshape(depth, D_MODEL, 2, N_GROUPS, LANE_W)
    wkv_t = jnp.transpose(wkv, (0, 3, 2, 4, 1)).reshape(depth, N_GROUPS, 2 * LANE_W, D_MODEL).astype(BF16)
    bkv = jnp.transpose(b_in[:, ATT_W:3 * ATT_W].reshape(depth, 2, N_GROUPS, LANE_W), (0, 2, 1, 3))
    bkv_t = jnp.broadcast_to(bkv.reshape(depth, N_GROUPS, 2 * LANE_W, 1), (depth, N_GROUPS, 2 * LANE_W, SPAD))

    mod_all = adaln_mod(jnp.concatenate([c_prompt, c_sample], axis=0), w_mod, b_mod)
    w_in_bf = w_in.astype(BF16)
    b_in3 = _row3(b_in)
    wr_pad = jnp.pad(w_router, ((0, 0), (0, 0), (0, ROUTER_LANES - N_EXPERTS)))
    wr_hi = wr_pad.astype(BF16)
    wr_lo = (wr_pad - wr_hi.astype(F32)).astype(BF16)
    br_pad = jnp.pad(b_router, ((0, 0), (0, ROUTER_LANES - N_EXPERTS)), constant_values=NEG_INF)
    mix_wts = (w_oa.astype(BF16), _row3(b_oa), w_pb.astype(BF16), _row3(b_pb), w_out.astype(BF16), _row3(b_out),
               _row3(ln1_g), _row3(ln1_b), wr_hi, wr_lo, _row3(br_pad))
    conv_b3, cg3, cb3 = _row3(conv_b), _row3(conv_ln_g), _row3(conv_ln_b)
    conv_w = jnp.repeat(conv_w, VREG_SUBLANES, axis=1)
    ln2g3, ln2b3 = _row3(ln2_g), _row3(ln2_b)
    e = w_up.shape[1]
    b_up4 = b_up.reshape(depth, e, 1, 2 * D_FF)
    b_down4 = b_down.reshape(depth, e, 1, D_MODEL)

    xp = x_prompt
    xs_ = x_sample.reshape(1, n_s, d)
    wp = [[] for _ in range(N_GROUPS)]
    ws = [jnp.zeros(c.shape, F32) for c in caches]
    cp, cs = [], []
    for l in range(depth):
        mod_p = mod_all[l, :b_].reshape(b_, 1, 6 * d)
        mod_s = jnp.repeat(mod_all[l, b_:], t_new, axis=0).reshape(1, n_s, 6 * d)

        pq0, pq1, pq2, pkv0, pkv1, pkv2, pu, pga, pgb = in_proj(xp, mod_p, w_in_bf, b_in3, l, tm_p, DILATIONS)
        pkv = (pkv0, pkv1, pkv2)
        po, pl_ = [], []
        for g, qg in enumerate((pq0, pq1, pq2)):
            o, lse = attn_prompt(qg, pkv[g], slopes[g * LANES:(g + 1) * LANES], DILATIONS[g])
            po.append(o)
            pl_.append(lse)
            tail = pkv[g][:, (s - WINDOWS[g]) // DILATIONS[g]:]
            wp[g].append(tail.reshape(b_, WINDOWS[g], 2, LANES, HEAD_DIM))
        pcb = conv_branch(pu, pu, conv_w, conv_b3, cg3, cb3, l, tm_p, True)
        cp.append(pu[:, s - (CONV_WIDTH - 1):])

        sq0, sq1, sq2, skv0, skv1, skv2, su, sga, sgb = in_proj(xs_, mod_s, w_in_bf, b_in3, l, n_s, no_fold)
        del sq0, sq1, sq2, skv0, skv1, skv2
        x8 = jnp.pad(xs_.reshape(db, t_new, d), ((0, 0), (SQ - t_new, 0), (0, 0)))
        mod_seq = mod_all[l, b_:].reshape(db, 1, 6 * d)
        so, sl_ = [], []
        for g in range(N_GROUPS):
            o, lse, ws[g] = attn_sample(x8, mod_seq, w_in_bf, b_in3, wkv_t, bkv_t, caches[g], ws[g], l, g,
                                        slopes[g * LANES:(g + 1) * LANES], DILATIONS[g], t_new)
            so.append(o[:, SQ - t_new:].reshape(1, n_s, LANE_W))
            sl_.append(lse[:, SQ - t_new:].reshape(1, n_s, LANE_W))
        su3 = su.reshape(db, t_new, CONV_CH)
        full_s = jnp.concatenate([state_conv[l], su3], axis=1)
        cs.append(full_s[:, t_new:])
        st_pad = jnp.pad(state_conv[l], ((0, 0), (CONV_HALO - (CONV_WIDTH - 1), 0), (0, 0)))
        su8 = jnp.pad(su3, ((0, 0), (0, SQ - t_new), (0, 0)))
        scb = conv_branch(su8, st_pad, conv_w, conv_b3, cg3, cb3, l, SQ, False)
        scb = scb[:, :t_new].reshape(1, n_s, CONV_CH)

        px1, phu, plg = mix(po, pl_, pcb, pga, pgb, xp, mod_p, mix_wts, l, tm_p, DILATIONS)
        sx1, shu, slg = mix(so, sl_, scb, sga, sgb, xs_, mod_s, mix_wts, l, n_s, no_fold)

        hu = jnp.concatenate([phu.reshape(n_p, ROW_SUB, ROW_LANE), shu.reshape(n_s, ROW_SUB, ROW_LANE)], axis=0)
        logits = jnp.concatenate([plg.reshape(n_p, ROUTER_LANES), slg.reshape(n_s, ROUTER_LANES)], axis=0)
        route, counts = router(logits)
        dest4, block_e, n_used, cap = routing_plan(route, counts, n_tok)
        xsort = dispatch_rows(hu, dest4, cap)
        ysort = moe_experts(xsort.reshape(cap * ROW_SUB, ROW_LANE), block_e, n_used,
                            w_up, b_up4, w_down, b_down4, l)
        n_pad = -(-n_tok // tm_p) * tm_p
        yg = gather_rows(ysort.reshape(cap, ROW_SUB, ROW_LANE), jnp.pad(dest4, ((0, 0), (0, n_pad - n_tok))))
        yg = yg.reshape(TOP_K * n_pad * ROW_SUB, ROW_LANE)

        xp = final(yg, route, px1, mod_p, ln2g3, ln2b3, l, tm_p, 0)
        xs_ = final(yg, route, sx1, mod_s, ln2g3, ln2b3, l, n_s, n_p)

    stack = lambda xs: jnp.stack(xs)
    win_s = [jnp.transpose(w.reshape(depth, db, 2, LANES, HEAD_DIM, w.shape[-1]), (0, 1, 5, 2, 3, 4)) for w in ws]
    return (xp, xs_.reshape(db, t_new, d),
            stack(wp[0]), stack(wp[1]), stack(wp[2]), stack(cp),
            win_s[0], win_s[1], win_s[2], stack(cs))
```

```python
import functools

import jax
import jax.numpy as jnp
from jax import lax
from jax.experimental import pallas as pl
from jax.experimental.pallas import tpu as pltpu
from jax.experimental.pallas import tpu_sc as plsc

F32 = jnp.float32
BF16 = jnp.bfloat16
U32 = jnp.uint32
I32 = jnp.int32

D_MODEL = 1024
N_GROUPS = 3
LANES = 4
HEAD_DIM = 64
LANE_W = LANES * HEAD_DIM
ATT_W = N_GROUPS * LANE_W
N_HEADS = N_GROUPS * LANES
WINDOWS = (128, 512, 2048)
DILATIONS = (1, 4, 16)
BAND = 128
CONV_CH = 512
CONV_WIDTH = 31
CONV_HALO = 32
N_EXPERTS = 32
TOP_K = 4
D_FF = 1024
SWIGLU_LIMIT = 7.0
SWIGLU_ALPHA = 1.702
N_IN = 3 * ATT_W + 2 * CONV_CH + 2 * D_MODEL
DEPTH_FOR_ALPHA = 4
ALPHA_DN = (2 * DEPTH_FOR_ALPHA) ** 0.25
LN_EPS = 1e-5
NEG_INF = -1e30

VREG_LANES = 128
VREG_SUBLANES = 8
QBLK = 128
ROUTER_LANES = VREG_LANES
MOE_TM = 512
PACK_W = D_MODEL // 2
ROW_LANE = VREG_LANES
ROW_SUB = PACK_W // ROW_LANE
VMEM_LIMIT = 56 * 1024 * 1024

_NT = (((1,), (1,)), ((), ()))


def _cparams(*sem):
    return pltpu.CompilerParams(dimension_semantics=sem, vmem_limit_bytes=VMEM_LIMIT)


def _sigmoid(x):
    return 1.0 / (1.0 + jnp.exp(-x))


def _layer_norm(r, g, b):
    mu = jnp.mean(r, axis=-1, keepdims=True)
    d = r - mu
    var = jnp.mean(d * d, axis=-1, keepdims=True)
    return d * lax.rsqrt(var + LN_EPS) * g + b


def _pack_bf16_pair(lo, hi):
    lo_b = lax.bitcast_convert_type(lo.astype(BF16).astype(F32), U32)
    hi_b = lax.bitcast_convert_type(hi.astype(BF16).astype(F32), U32)
    return (lo_b >> 16) | (hi_b & jnp.uint32(0xFFFF0000))


def _pack_bf16_pair_native(lo, hi):
    return pltpu.pack_elementwise([lo, hi], packed_dtype=BF16)


def _unpack_bf16_pair(w):
    lo = lax.bitcast_convert_type(w << 16, F32)
    hi = lax.bitcast_convert_type(w & jnp.uint32(0xFFFF0000), F32)
    return lo, hi


def _store_row_words(ref, words):
    rows = words.shape[0]
    for j in range(ROW_SUB):
        ref[pl.ds(j, rows, stride=ROW_SUB), :] = words[:, j * ROW_LANE:(j + 1) * ROW_LANE]


def _load_row_words(ref):
    rows = ref.shape[0] // ROW_SUB
    return jnp.concatenate([ref[pl.ds(j, rows, stride=ROW_SUB), :] for j in range(ROW_SUB)], axis=1)


def _mod_spec(mod, tm, width, col_block):
    if mod.shape[1] == 1:
        return pl.BlockSpec((1, 1, width), lambda g, i: (g, 0, col_block))
    return pl.BlockSpec((1, tm, width), lambda g, i: (g, i, col_block))


def _layer_spec(a, layer):
    return pl.BlockSpec((1,) + a.shape[1:], lambda g, i: (layer,) + (0,) * (a.ndim - 1))


def _mod_kernel(c_ref, w_ref, b_ref, o_ref):
    c = c_ref[...]
    a = (c * _sigmoid(c)).astype(BF16)
    o_ref[0] = jnp.dot(a, w_ref[0].astype(BF16), preferred_element_type=F32) + b_ref[0]


def adaln_mod(c_all, w_mod, b_mod):
    depth, d, n = w_mod.shape
    m = c_all.shape[0]
    tn = 1536
    return pl.pallas_call(
        _mod_kernel,
        grid=(depth, n // tn),
        in_specs=[pl.BlockSpec((m, d), lambda l, j: (0, 0)),
                  pl.BlockSpec((1, d, tn), lambda l, j: (l, 0, j)),
                  pl.BlockSpec((1, 1, tn), lambda l, j: (l, 0, j))],
        out_specs=pl.BlockSpec((1, m, tn), lambda l, j: (l, 0, j)),
        out_shape=jax.ShapeDtypeStruct((depth, m, n), F32),
        compiler_params=_cparams("arbitrary", "arbitrary"),
        name="adaln_mod",
    )(c_all, w_mod, b_mod.reshape(depth, 1, n))


def _in_proj_kernel(*refs, dils, windows):
    if windows is None:
        (x_ref, mod_ref, w_ref, b_ref, q0, q1, q2, kv0, kv1, kv2, u_ref, ga_ref, gb_ref, zs_ref) = refs
    else:
        (x_ref, mod_ref, w_ref, b_ref, wkv_ref, bkv_ref, q0, q1, q2, kv0, kv1, kv2, u_ref, ga_ref, gb_ref,
         t0, t1, t2, zs_ref) = refs
    x = x_ref[0]
    tm = x.shape[0]
    sh = mod_ref[0, :, 0:D_MODEL]
    sc = mod_ref[0, :, D_MODEL:2 * D_MODEL]
    h = (x * (1.0 + sc) + sh).astype(BF16)

    if windows is not None:
        i = pl.program_id(1)
        n_tiles = pl.num_programs(1)
        for g, t_ref in enumerate((t0, t1, t2)):
            @pl.when(i >= n_tiles - max(windows[g] // tm, 1))
            def _(g=g, t_ref=t_ref):
                bias = jnp.concatenate([bkv_ref[0, g]] * (tm // SPAD), axis=1)
                kvt = lax.dot_general(wkv_ref[0, g], h, _NT, preferred_element_type=F32) + bias
                t_ref[0] = kvt if windows[g] >= tm else kvt[:, tm - windows[g]:]

    def proj(c0, c1):
        return jnp.dot(h, w_ref[0, :, c0:c1], preferred_element_type=F32) + b_ref[0, :, c0:c1]

    def put(out_ref, col0, z, dil):
        if dil == 1:
            out_ref[0, :, col0:col0 + LANE_W] = z.astype(out_ref.dtype)
            return
        nt = LANE_W // VREG_LANES
        for c in range(nt):
            zs_ref[c] = z[:, c * VREG_LANES:(c + 1) * VREG_LANES]
        blk = out_ref.shape[2] // dil
        for r in range(dil):
            for c in range(nt):
                piece = zs_ref[c, pl.ds(r, tm // dil, stride=dil), :]
                lo = r * blk + col0 + c * VREG_LANES
                out_ref[0, :, lo:lo + VREG_LANES] = piece.astype(out_ref.dtype)

    for g, (q_ref, kv_ref) in enumerate(((q0, kv0), (q1, kv1), (q2, kv2))):
        c = g * LANE_W
        put(q_ref, 0, proj(c, c + LANE_W) * (HEAD_DIM ** -0.5), dils[g])
        put(kv_ref, 0, proj(ATT_W + c, ATT_W + c + LANE_W), dils[g])
        put(kv_ref, LANE_W, proj(2 * ATT_W + c, 2 * ATT_W + c + LANE_W), dils[g])
    o1 = 3 * ATT_W
    glu_a = proj(o1, o1 + CONV_CH)
    glu_b = proj(o1 + CONV_CH, o1 + 2 * CONV_CH)
    u_ref[0] = glu_a * _sigmoid(glu_b)
    o2 = o1 + 2 * CONV_CH
    ga_ref[0] = _sigmoid(proj(o2, o2 + D_MODEL)).astype(BF16)
    gb_ref[0] = _sigmoid(proj(o2 + D_MODEL, o2 + 2 * D_MODEL)).astype(BF16)


def in_proj(x, mod, w_bf, b, layer, tm, dils, wkv_t=None, bkv_t=None, windows=None):
    g_, r, d = x.shape
    n_tiles = r // tm
    row = lambda w: pl.BlockSpec((1, tm, w), lambda g, i: (g, i, 0))
    fold = lambda w, dil: pl.BlockSpec((1, tm // dil, dil * w), lambda g, i: (g, i, 0))
    sds = lambda w, dt: jax.ShapeDtypeStruct((g_, r, w), dt)
    fsds = lambda w, dil, dt: jax.ShapeDtypeStruct((g_, r // dil, dil * w), dt)
    in_specs = [row(d), _mod_spec(mod, tm, 2 * d, 0), _layer_spec(w_bf, layer), _layer_spec(b, layer)]
    args = [x, mod, w_bf, b]
    out_specs = ([fold(LANE_W, dl) for dl in dils] + [fold(2 * LANE_W, dl) for dl in dils]
                 + [row(CONV_CH), row(d), row(d)])
    out_shape = ([fsds(LANE_W, dl, BF16) for dl in dils] + [fsds(2 * LANE_W, dl, F32) for dl in dils]
                 + [sds(CONV_CH, F32), sds(d, BF16), sds(d, BF16)])
    if windows is not None:
        in_specs += [_layer_spec(wkv_t, layer), _layer_spec(bkv_t, layer)]
        args += [wkv_t, bkv_t]
        for w in windows:
            assert w % tm == 0 or tm % w == 0
            first = n_tiles - max(w // tm, 1)
            out_specs.append(pl.BlockSpec((1, 2 * LANE_W, min(w, tm)),
                                          lambda g, i, first=first: (g, 0, jnp.maximum(i - first, 0))))
            out_shape.append(jax.ShapeDtypeStruct((g_, 2 * LANE_W, w), F32))
    return pl.pallas_call(
        functools.partial(_in_proj_kernel, dils=dils, windows=windows),
        grid=(g_, n_tiles),
        in_specs=in_specs,
        out_specs=out_specs,
        out_shape=out_shape,
        scratch_shapes=[pltpu.VMEM((LANE_W // VREG_LANES, tm, VREG_LANES), F32)],
        compiler_params=_cparams("arbitrary", "arbitrary"),
        name="in_proj",
    )(*args)


def _attn_kernel(q_ref, kvp_ref, kvc_ref, bias_ref, o_ref, l_ref, *, qb):
    i = pl.program_id(2)
    head_of_lane = lax.broadcasted_iota(I32, (QBLK, LANE_W), 1) >> 6
    ones = jnp.ones((2 * QBLK, VREG_LANES), BF16)
    first_cols = lax.broadcasted_iota(I32, (LANES * QBLK, 2 * QBLK), 1) < QBLK
    for j in range(qb):
        rs = slice(j * QBLK, (j + 1) * QBLK)
        q = q_ref[0, rs, :]
        q4 = jnp.concatenate([jnp.where(head_of_lane == h, q, jnp.zeros_like(q)) for h in range(LANES)], axis=0)
        if j == 0:
            kv = jnp.concatenate([kvp_ref[0], kvc_ref[0, rs, :]], axis=0)
        else:
            kv = kvc_ref[0, (j - 1) * QBLK:(j + 1) * QBLK, :]
        k2, v2 = kv[:, 0:LANE_W].astype(BF16), kv[:, LANE_W:2 * LANE_W].astype(BF16)
        s = lax.dot_general(q4, k2, _NT, preferred_element_type=F32) + bias_ref[...]
        if j == 0:
            s = jnp.where(jnp.logical_and(first_cols, i == 0), NEG_INF, s)
        m = jnp.max(s, axis=-1, keepdims=True)
        p = jnp.exp(s - m).astype(BF16)
        l = jnp.dot(p, ones, preferred_element_type=F32)
        o4 = jnp.dot(p, v2, preferred_element_type=F32)
        inv = 1.0 / l
        o4 = o4 * jnp.concatenate([inv, inv], axis=1)
        lse = m + jnp.log(l)
        lse2 = jnp.concatenate([lse, lse], axis=1)
        o, ls = o4[0:QBLK], lse2[0:QBLK]
        for h in range(1, LANES):
            sel = head_of_lane == h
            o = jnp.where(sel, o4[h * QBLK:(h + 1) * QBLK], o)
            ls = jnp.where(sel, lse2[h * QBLK:(h + 1) * QBLK], ls)
        o_ref[0, rs, :] = o.astype(BF16)
        l_ref[0, rs, :] = ls


def _band_bias(slopes_g, dil):
    qi = jnp.arange(QBLK)[:, None]
    kj = jnp.arange(QBLK)[None, :]
    dist_a = qi + BAND - kj
    dist_b = qi - kj
    sl = -(slopes_g * dil)[:, None, None]
    ba = jnp.where((dist_a <= BAND)[None], sl * dist_a.astype(F32)[None], NEG_INF)
    bb = jnp.where((dist_b >= 0)[None], sl * dist_b.astype(F32)[None], NEG_INF)
    return jnp.concatenate([ba, bb], axis=2).astype(F32).reshape(LANES * QBLK, 2 * QBLK)


def attn_prompt(qf, kvf, slopes_g, dil):
    b_, l_, _ = qf.shape
    qb = 4 if l_ % (4 * QBLK) == 0 else (2 if l_ % (2 * QBLK) == 0 else 1)
    nb = l_ // (qb * QBLK)
    bias = _band_bias(slopes_g, dil)
    bias_spec = pl.BlockSpec((LANES * QBLK, 2 * QBLK), lambda b, r, i: (0, 0))
    return pl.pallas_call(
        functools.partial(_attn_kernel, qb=qb),
        grid=(b_, dil, nb),
        in_specs=[pl.BlockSpec((1, qb * QBLK, LANE_W), lambda b, r, i: (b, i, r)),
                  pl.BlockSpec((1, QBLK, 2 * LANE_W), lambda b, r, i: (b, jnp.maximum(i * qb - 1, 0), r)),
                  pl.BlockSpec((1, qb * QBLK, 2 * LANE_W), lambda b, r, i: (b, i, r)),
                  bias_spec],
        out_specs=[pl.BlockSpec((1, qb * QBLK, LANE_W), lambda b, r, i: (b, i, r))] * 2,
        out_shape=[jax.ShapeDtypeStruct((b_, l_, dil * LANE_W), BF16),
                   jax.ShapeDtypeStruct((b_, l_, dil * LANE_W), F32)],
        compiler_params=_cparams("arbitrary", "arbitrary", "arbitrary"),
        name="attn_prompt",
    )(qf, kvf, kvf, bias)


SQ = 8
SPAD = 128


def _attn_s_kernel(x_ref, mod_ref, wq_ref, bq_ref, wkv_ref, bkv_ref, cache_ref, bias_ref, _tails_in,
                   o_ref, l_ref, tail_ref, hpad_ref, *, lw, t_new):
    nq = LANES * SQ
    blk = 2 * SQ

    @pl.when(pl.program_id(0) == 0)
    def _():
        hpad_ref[0:SPAD - blk, :] = jnp.zeros((SPAD - blk, D_MODEL), BF16)

    sh = mod_ref[0, :, 0:D_MODEL]
    sc = mod_ref[0, :, D_MODEL:2 * D_MODEL]
    h8 = x_ref[0] * (1.0 + sc) + sh
    h16 = jnp.concatenate([jnp.zeros_like(h8), h8], axis=0).astype(BF16)
    hpad_ref[SPAD - blk:SPAD, :] = h16
    q16 = (jnp.dot(h16, wq_ref[0], preferred_element_type=F32) + bq_ref[0]) * (HEAD_DIM ** -0.5)
    q8 = q16[SQ:blk].astype(BF16)
    head_of_lane = lax.broadcasted_iota(I32, (SQ, LANE_W), 1) >> 6
    q4 = jnp.concatenate([jnp.where(head_of_lane == h, q8, jnp.zeros_like(q8)) for h in range(LANES)], axis=0)
    kvn = lax.dot_general(wkv_ref[0, 0], hpad_ref[...], _NT, preferred_element_type=F32) + bkv_ref[0, 0]
    kc = cache_ref[0, 0, 0:LANE_W, :].astype(BF16)
    vc = cache_ref[0, 0, LANE_W:2 * LANE_W, :].astype(BF16)
    s_c = jnp.dot(q4, kc, preferred_element_type=F32) + bias_ref[:, 0:lw]
    s_n = jnp.dot(q4, kvn[0:LANE_W].astype(BF16), preferred_element_type=F32) + bias_ref[:, lw:lw + SPAD]
    m = jnp.maximum(jnp.max(s_c, axis=-1, keepdims=True), jnp.max(s_n, axis=-1, keepdims=True))
    p_c = jnp.exp(s_c - m)
    p_n = jnp.exp(s_n - m)
    l = jnp.sum(p_c, axis=-1, keepdims=True) + jnp.sum(p_n, axis=-1, keepdims=True)
    o4 = (lax.dot_general(p_c.astype(BF16), vc, _NT, preferred_element_type=F32)
          + lax.dot_general(p_n.astype(BF16), kvn[LANE_W:2 * LANE_W].astype(BF16), _NT,
                            preferred_element_type=F32)) / l
    lf = jnp.broadcast_to(m + jnp.log(l), (nq, LANE_W))
    rows = lax.broadcasted_iota(I32, (nq, LANE_W), 0)
    lanes = lax.broadcasted_iota(I32, (nq, LANE_W), 1)
    own = (rows >> 3) == (lanes >> 6)
    o4 = jnp.where(own, o4, 0.0)
    lf = jnp.where(own, lf, 0.0)
    o_ref[0] = (o4[0:SQ] + o4[SQ:2 * SQ] + o4[2 * SQ:3 * SQ] + o4[3 * SQ:4 * SQ]).astype(BF16)
    l_ref[0] = lf[0:SQ] + lf[SQ:2 * SQ] + lf[2 * SQ:3 * SQ] + lf[3 * SQ:4 * SQ]
    rolled = pltpu.roll(cache_ref[0, 0], lw - t_new, 1)
    if lw > SPAD:
        tail_ref[0, 0, :, 0:lw - SPAD] = rolled[:, 0:lw - SPAD]
    lane = lax.broadcasted_iota(I32, (2 * LANE_W, SPAD), 1)
    tail_ref[0, 0, :, lw - SPAD:lw] = jnp.where(lane >= SPAD - t_new, kvn, rolled[:, lw - SPAD:lw])


def _sample_bias(slopes_g, dil, lw, t_new):
    t = jnp.maximum(jnp.arange(SQ) - (SQ - t_new), 0)[:, None]
    r = jnp.arange(lw + SPAD)[None, :]
    pos = jnp.where(r < lw, r, r - (SPAD - t_new))
    dist = lw + t - pos
    jn = dist // dil
    real = (r < lw) | (r >= lw + SPAD - t_new)
    valid = real & (dist >= 0) & (dist % dil == 0) & (jn <= BAND)
    sl = (slopes_g * dil)[:, None, None]
    bias = jnp.where(valid[None], -sl * jn.astype(F32)[None], NEG_INF)
    return bias.reshape(LANES * SQ, lw + SPAD).astype(F32)


def attn_sample(x8, mod, w_bf, b3, wkv_t, bkv_t, cache_t, tails, layer, g, slopes_g, dil, t_new):
    db = x8.shape[0]
    depth, _, ch, lw = cache_t.shape
    bias = _sample_bias(slopes_g, dil, lw, t_new)
    in_specs = [pl.BlockSpec((1, SQ, D_MODEL), lambda b: (b, 0, 0)),
                pl.BlockSpec((1, 1, 2 * D_MODEL), lambda b: (b, 0, 0)),
                pl.BlockSpec((1, D_MODEL, LANE_W), lambda b: (layer, 0, g)),
                pl.BlockSpec((1, 1, LANE_W), lambda b: (layer, 0, g)),
                pl.BlockSpec((1, 1, ch, D_MODEL), lambda b: (layer, g, 0, 0)),
                pl.BlockSpec((1, 1, ch, SPAD), lambda b: (layer, g, 0, 0)),
                pl.BlockSpec((1, 1, ch, lw), lambda b: (layer, b, 0, 0)),
                pl.BlockSpec((LANES * SQ, lw + SPAD), lambda b: (0, 0)),
                pl.BlockSpec(memory_space=pl.ANY)]
    args = [x8, mod, w_bf, b3, wkv_t, bkv_t, cache_t, bias, tails]
    aliases = {len(args) - 1: 2}
    return pl.pallas_call(
        functools.partial(_attn_s_kernel, lw=lw, t_new=t_new),
        grid=(db,),
        in_specs=in_specs,
        out_specs=[pl.BlockSpec((1, SQ, LANE_W), lambda b: (b, 0, 0)),
                   pl.BlockSpec((1, SQ, LANE_W), lambda b: (b, 0, 0)),
                   pl.BlockSpec((1, 1, ch, lw), lambda b: (layer, b, 0, 0))],
        out_shape=[jax.ShapeDtypeStruct((db, SQ, LANE_W), BF16),
                   jax.ShapeDtypeStruct((db, SQ, LANE_W), F32),
                   jax.ShapeDtypeStruct((depth, db, ch, lw), F32)],
        scratch_shapes=[pltpu.VMEM((SPAD, D_MODEL), BF16)],
        input_output_aliases=aliases,
        compiler_params=_cparams("arbitrary"),
        name="attn_sample",
    )(*args)


CONV_ROWS = 16


def _conv_kernel(prev_ref, cur_ref, w_ref, cb_ref, g_ref, b_ref, o_ref, sh_ref, *, zero_first):
    t_rows = cur_ref.shape[1]
    n = CONV_HALO + t_rows
    prev = prev_ref[0]
    if zero_first:
        prev = jnp.where(pl.program_id(1) == 0, 0.0, prev)
    sh_ref[0, 0:CONV_HALO, :] = prev
    sh_ref[0, CONV_HALO:n, :] = cur_ref[0]
    for s in range(1, VREG_SUBLANES):
        sh_ref[s, 0:n - VREG_SUBLANES, :] = sh_ref[0, pl.ds(s, n - VREG_SUBLANES), :]
    rows = min(CONV_ROWS, t_rows)
    off = CONV_HALO - (CONV_WIDTH - 1)
    ngrp = rows // VREG_SUBLANES
    for r0 in range(0, t_rows, rows):
        accs = [None] * ngrp
        for w in range(CONV_WIDTH):
            s = (w + off) % VREG_SUBLANES
            a = r0 + (w + off) - s
            w8 = w_ref[0, w * VREG_SUBLANES:(w + 1) * VREG_SUBLANES, :]
            for k in range(ngrp):
                lo = a + k * VREG_SUBLANES
                term = sh_ref[s, lo:lo + VREG_SUBLANES, :] * w8
                accs[k] = term if accs[k] is None else accs[k] + term
        acc = jnp.concatenate(accs, axis=0) if ngrp > 1 else accs[0]
        z = _layer_norm(acc + cb_ref[0], g_ref[0], b_ref[0])
        o_ref[0, r0:r0 + rows, :] = (z * _sigmoid(z)).astype(BF16)


def conv_branch(u, prev_src, conv_w, conv_b, ln_g, ln_b, layer, tm, zero_first):
    g_, r, c = u.shape
    per = tm // CONV_HALO
    if zero_first:
        prev_map = lambda g, i: (g, jnp.maximum(i * per - 1, 0), 0)
    else:
        prev_map = lambda g, i: (g, 0, 0)
    return pl.pallas_call(
        functools.partial(_conv_kernel, zero_first=zero_first),
        grid=(g_, r // tm),
        in_specs=[pl.BlockSpec((1, CONV_HALO, c), prev_map),
                  pl.BlockSpec((1, tm, c), lambda g, i: (g, i, 0)),
                  _layer_spec(conv_w, layer), _layer_spec(conv_b, layer),
                  _layer_spec(ln_g, layer), _layer_spec(ln_b, layer)],
        out_specs=pl.BlockSpec((1, tm, c), lambda g, i: (g, i, 0)),
        out_shape=jax.ShapeDtypeStruct((g_, r, c), BF16),
        scratch_shapes=[pltpu.VMEM((VREG_SUBLANES, CONV_HALO + tm, c), F32)],
        compiler_params=_cparams("arbitrary", "arbitrary"),
        name="conv_branch",
    )(prev_src, u, conv_w, conv_b, ln_g, ln_b)


def _mix_kernel(o0, o1, o2, l0, l1, l2, cb_ref, ga_ref, gb_ref, x_ref, mod_ref,
                woa_ref, boa_ref, wpb_ref, bpb_ref, wout_ref, bout_ref, g1_ref, b1_ref,
                wrh_ref, wrl_ref, br_ref, x1_ref, hu_ref, lg_ref, s0, s1, s2, s3, *, dils):
    tm = x_ref.shape[1]

    def unfold(ref, dil, scratch):
        if dil == 1:
            return ref[0].astype(F32)
        nt = LANE_W // VREG_LANES
        for r in range(dil):
            for c in range(nt):
                lo = r * LANE_W + c * VREG_LANES
                scratch[c, pl.ds(r, tm // dil, stride=dil), :] = ref[0, :, lo:lo + VREG_LANES].astype(F32)
        return jnp.concatenate([scratch[c] for c in range(nt)], axis=1)

    assert dils[0] == 1
    oa, ob, oc = unfold(o0, 1, None), unfold(o1, dils[1], s0), unfold(o2, dils[2], s1)
    la, lb, lc = unfold(l0, 1, None), unfold(l1, dils[1], s2), unfold(l2, dils[2], s3)
    lmax = jnp.maximum(jnp.maximum(la, lb), lc)
    ea, eb, ec = jnp.exp(la - lmax), jnp.exp(lb - lmax), jnp.exp(lc - lmax)
    merged = (ea * oa + eb * ob + ec * oc) / (ea + eb + ec)
    br_a = jnp.dot(merged.astype(BF16), woa_ref[0], preferred_element_type=F32) + boa_ref[0]
    br_b = jnp.dot(cb_ref[0], wpb_ref[0], preferred_element_type=F32) + bpb_ref[0]
    mixed = ga_ref[0].astype(F32) * br_a + gb_ref[0].astype(F32) * br_b
    y = jnp.dot(mixed.astype(BF16), wout_ref[0], preferred_element_type=F32) + bout_ref[0]
    gt1 = mod_ref[0, :, 2 * D_MODEL:3 * D_MODEL]
    sh2 = mod_ref[0, :, 3 * D_MODEL:4 * D_MODEL]
    sc2 = mod_ref[0, :, 4 * D_MODEL:5 * D_MODEL]
    x1 = _layer_norm(ALPHA_DN * x_ref[0] + (1.0 + gt1) * y, g1_ref[0], b1_ref[0])
    x1_ref[0] = x1
    h2 = x1 * (1.0 + sc2) + sh2
    hi = h2.astype(BF16)
    lo = (h2 - hi.astype(F32)).astype(BF16)
    lg_ref[0] = (jnp.dot(hi, wrh_ref[0], preferred_element_type=F32)
                 + jnp.dot(lo, wrh_ref[0], preferred_element_type=F32)
                 + jnp.dot(hi, wrl_ref[0], preferred_element_type=F32) + br_ref[0])
    _store_row_words(hu_ref.at[0], _pack_bf16_pair(h2[:, 0:PACK_W], h2[:, PACK_W:D_MODEL]))


def mix(o_g, l_g, cb, ga, gb, x, mod, wts, layer, tm, dils):
    g_, r, d = x.shape
    row = lambda w: pl.BlockSpec((1, tm, w), lambda g, i: (g, i, 0))
    fold = lambda dil: pl.BlockSpec((1, tm // dil, dil * LANE_W), lambda g, i: (g, i, 0))
    return pl.pallas_call(
        functools.partial(_mix_kernel, dils=dils),
        grid=(g_, r // tm),
        in_specs=[fold(dl) for dl in dils] * 2 + [row(CONV_CH), row(d), row(d), row(d), _mod_spec(mod, tm, 6 * d, 0)]
                 + [_layer_spec(a, layer) for a in wts],
        out_specs=[row(d), pl.BlockSpec((1, tm * ROW_SUB, ROW_LANE), lambda g, i: (g, i, 0)), row(ROUTER_LANES)],
        out_shape=[jax.ShapeDtypeStruct((g_, r, d), F32),
                   jax.ShapeDtypeStruct((g_, r * ROW_SUB, ROW_LANE), U32),
                   jax.ShapeDtypeStruct((g_, r, ROUTER_LANES), F32)],
        scratch_shapes=[pltpu.VMEM((LANE_W // VREG_LANES, tm, VREG_LANES), F32)] * 4,
        compiler_params=_cparams("arbitrary", "arbitrary"),
        name="mix",
    )(*o_g, *l_g, cb, ga, gb, x, mod, *wts)


def _router_kernel(lg_ref, tri_ref, cin_ref, out_ref, cnt_ref, carry_ref):
    @pl.when(pl.program_id(0) == 0)
    def _():
        carry_ref[...] = cin_ref[...]

    work = lg_ref[...]
    tm = work.shape[0]
    lane = lax.broadcasted_iota(I32, (tm, ROUTER_LANES), 1)
    vals, idxs, hots = [], [], []
    for _ in range(TOP_K):
        m = jnp.max(work, axis=-1, keepdims=True)
        idx = jnp.min(jnp.where(work == m, lane, ROUTER_LANES), axis=-1, keepdims=True)
        hot = lane == idx
        work = jnp.where(hot, -jnp.inf, work)
        vals.append(m)
        idxs.append(idx)
        hots.append(hot)
    es = [jnp.exp(v - vals[0]) for v in vals]
    den = es[0] + es[1] + es[2] + es[3]
    member = jnp.where(hots[0] | hots[1] | hots[2] | hots[3], 1.0, 0.0)
    before = jnp.dot(tri_ref[...], member.astype(BF16), preferred_element_type=F32) + carry_ref[...]
    out = jnp.zeros((tm, ROUTER_LANES), F32)
    for k in range(TOP_K):
        rank = jnp.sum(jnp.where(hots[k], before, 0.0), axis=-1, keepdims=True)
        out = jnp.where(lane == k, es[k] / den, out)
        out = jnp.where(lane == TOP_K + k, idxs[k].astype(F32), out)
        out = jnp.where(lane == 2 * TOP_K + k, rank, out)
    out_ref[...] = out
    carry_ref[...] = carry_ref[...] + jnp.sum(member, axis=0, keepdims=True)
    cnt_ref[...] = carry_ref[...]


def router(logits, counts_in):
    n = logits.shape[0]
    tm = next(t for t in (512, 384, 256, 128) if n % t == 0)
    tri = (jnp.arange(tm)[None, :] < jnp.arange(tm)[:, None]).astype(BF16)
    return pl.pallas_call(
        _router_kernel,
        grid=(n // tm,),
        in_specs=[pl.BlockSpec((tm, ROUTER_LANES), lambda i: (i, 0)),
                  pl.BlockSpec((tm, tm), lambda i: (0, 0)),
                  pl.BlockSpec((1, ROUTER_LANES), lambda i: (0, 0))],
        out_specs=[pl.BlockSpec((tm, ROUTER_LANES), lambda i: (i, 0)),
                   pl.BlockSpec((1, ROUTER_LANES), lambda i: (0, 0))],
        out_shape=[jax.ShapeDtypeStruct((n, ROUTER_LANES), F32),
                   jax.ShapeDtypeStruct((1, ROUTER_LANES), F32)],
        scratch_shapes=[pltpu.VMEM((1, ROUTER_LANES), F32)],
        compiler_params=_cparams("arbitrary"),
        name="router",
    )(logits, tri, counts_in)


def _moe_kernel(be_ref, nu_ref, xs_ref, wup_ref, bup_ref, wdn_ref, bdn_ref, y_ref, wup_bf, wdn_bf, act_ref):
    i = pl.program_id(0)

    @pl.when(i < nu_ref[0])
    def _():
        prev_e = be_ref[jnp.maximum(i - 1, 0)]

        @pl.when((i == 0) | (be_ref[i] != prev_e))
        def _():
            step = 128
            for r0 in range(0, D_MODEL, step):
                wup_bf[r0:r0 + step, :] = wup_ref[0, 0, r0:r0 + step, :].astype(BF16)
            for r0 in range(0, D_FF, step):
                wdn_bf[r0:r0 + step, :] = wdn_ref[0, 0, r0:r0 + step, :].astype(BF16)

        lo, hi = _unpack_bf16_pair(_load_row_words(xs_ref))
        x = jnp.concatenate([lo, hi], axis=1).astype(BF16)
        cw = 512
        for c0 in range(0, D_FF, cw):
            a = jnp.dot(x, wup_bf[:, c0:c0 + cw], preferred_element_type=F32) + bup_ref[0, 0, :, c0:c0 + cw]
            lin = (jnp.dot(x, wup_bf[:, D_FF + c0:D_FF + c0 + cw], preferred_element_type=F32)
                   + bup_ref[0, 0, :, D_FF + c0:D_FF + c0 + cw])
            a = jnp.minimum(a, SWIGLU_LIMIT)
            lin = jnp.clip(lin, -SWIGLU_LIMIT, SWIGLU_LIMIT)
            act_ref[:, c0:c0 + cw] = (a * _sigmoid(SWIGLU_ALPHA * a) * (lin + 1.0)).astype(BF16)
        y = jnp.dot(act_ref[...], wdn_bf[...], preferred_element_type=F32) + bdn_ref[0, 0]
        _store_row_words(y_ref, _pack_bf16_pair_native(y[:, 0:PACK_W], y[:, PACK_W:D_MODEL]))


def moe_experts(xs, block_e, n_used, w_up, b_up, w_down, b_down, layer):
    cap = xs.shape[0] // ROW_SUB
    nblk = cap // MOE_TM
    row_map = lambda i, be, nu: (jnp.minimum(i, nu[0] - 1), 0)
    exp_map = lambda i, be, nu: (layer, be[i], 0, 0)
    grid_spec = pltpu.PrefetchScalarGridSpec(
        num_scalar_prefetch=2,
        grid=(nblk,),
        in_specs=[pl.BlockSpec((MOE_TM * ROW_SUB, ROW_LANE), row_map),
                  pl.BlockSpec((1, 1, D_MODEL, 2 * D_FF), exp_map),
                  pl.BlockSpec((1, 1, 1, 2 * D_FF), exp_map),
                  pl.BlockSpec((1, 1, D_FF, D_MODEL), exp_map),
                  pl.BlockSpec((1, 1, 1, D_MODEL), exp_map)],
        out_specs=pl.BlockSpec((MOE_TM * ROW_SUB, ROW_LANE), row_map),
        scratch_shapes=[pltpu.VMEM((D_MODEL, 2 * D_FF), BF16),
                        pltpu.VMEM((D_FF, D_MODEL), BF16),
                        pltpu.VMEM((MOE_TM, D_FF), BF16)],
    )
    return pl.pallas_call(
        _moe_kernel,
        grid_spec=grid_spec,
        out_shape=jax.ShapeDtypeStruct((cap * ROW_SUB, ROW_LANE), U32),
        compiler_params=_cparams("arbitrary"),
        name="moe_experts",
    )(block_e, n_used, xs, w_up, b_up, w_down, b_down)


def _final_kernel(y0, y1, y2, y3, r_ref, x_ref, mod_ref, g2_ref, b2_ref, o_ref):
    gates = r_ref[...]
    acc_lo = None
    for k, y_ref in enumerate((y0, y1, y2, y3)):
        lo, hi = _unpack_bf16_pair(_load_row_words(y_ref))
        gk = gates[:, k:k + 1]
        acc_lo = gk * lo if acc_lo is None else acc_lo + gk * lo
        acc_hi = gk * hi if k == 0 else acc_hi + gk * hi
    y = jnp.concatenate([acc_lo, acc_hi], axis=1)
    gt2 = mod_ref[0]
    o_ref[0] = _layer_norm(ALPHA_DN * x_ref[0] + (1.0 + gt2) * y, g2_ref[0], b2_ref[0])


def final(yg, route, x1, mod, ln_g, ln_b, layer, tm, row0):
    g_, r, d = x1.shape
    per = r // tm
    base = row0 // tm
    nblk_k = yg.shape[0] // (TOP_K * tm * ROW_SUB)
    yspec = lambda k: pl.BlockSpec((tm * ROW_SUB, ROW_LANE), lambda g, i: (k * nblk_k + base + g * per + i, 0))
    return pl.pallas_call(
        _final_kernel,
        grid=(g_, per),
        in_specs=[yspec(0), yspec(1), yspec(2), yspec(3),
                  pl.BlockSpec((tm, ROUTER_LANES), lambda g, i: (g * per + i, 0)),
                  pl.BlockSpec((1, tm, d), lambda g, i: (g, i, 0)), _mod_spec(mod, tm, d, 5),
                  _layer_spec(ln_g, layer), _layer_spec(ln_b, layer)],
        out_specs=pl.BlockSpec((1, tm, d), lambda g, i: (g, i, 0)),
        out_shape=jax.ShapeDtypeStruct((g_, r, d), F32),
        compiler_params=_cparams("arbitrary", "arbitrary"),
        name="final",
    )(yg, yg, yg, yg, route, x1, mod, ln_g, ln_b)


SC_CORES = 2
SC_SUBCORES = 16
SC_WORKERS = SC_CORES * SC_SUBCORES
SC_CHUNK = 128


def _sc_mesh():
    return plsc.VectorSubcoreMesh(core_axis_name="c", subcore_axis_name="s")


def _sc_worker_chunks(nchunk):
    wid = lax.axis_index("s") * SC_CORES + lax.axis_index("c")
    return wid, (nchunk - wid + SC_WORKERS - 1) // SC_WORKERS


def dispatch_rows(hu_a, hu_b, dest4, cap):
    chunks_a = hu_a.shape[0] // SC_CHUNK
    nchunk = chunks_a + hu_b.shape[0] // SC_CHUNK
    idx = dest4.reshape(TOP_K, nchunk, SC_CHUNK)

    @functools.partial(
        pl.kernel, mesh=_sc_mesh(),
        out_type=jax.ShapeDtypeStruct((cap, ROW_SUB, ROW_LANE), U32),
        scratch_types=[pltpu.VMEM((SC_CHUNK,), I32)] * TOP_K
                      + [pltpu.VMEM((SC_CHUNK, ROW_SUB, ROW_LANE), U32)])
    def k(a_hbm, b_hbm, i_hbm, o_hbm, i0, i1, i2, i3, rows_v):
        wid, n_mine = _sc_worker_chunks(nchunk)

        @pl.loop(0, n_mine)
        def _(j):
            c = wid + j * SC_WORKERS

            @pl.when(c < chunks_a)
            def _():
                pltpu.sync_copy(a_hbm.at[pl.ds(c * SC_CHUNK, SC_CHUNK)], rows_v)

            @pl.when(c >= chunks_a)
            def _():
                pltpu.sync_copy(b_hbm.at[pl.ds((c - chunks_a) * SC_CHUNK, SC_CHUNK)], rows_v)

            for kk, iv in enumerate((i0, i1, i2, i3)):
                pltpu.sync_copy(i_hbm.at[kk, c], iv)
            for iv in (i0, i1, i2, i3):
                pltpu.sync_copy(rows_v, o_hbm.at[iv])

    return k(hu_a, hu_b, idx)


def gather_rows(y, dest4):
    m = dest4.shape[0] * dest4.shape[1]
    nchunk = m // SC_CHUNK
    idx = dest4.reshape(nchunk, SC_CHUNK)

    @functools.partial(
        pl.kernel, mesh=_sc_mesh(),
        out_type=jax.ShapeDtypeStruct((m, ROW_SUB, ROW_LANE), U32),
        scratch_types=[pltpu.VMEM((SC_CHUNK,), I32), pltpu.VMEM((SC_CHUNK, ROW_SUB, ROW_LANE), U32)])
    def k(y_hbm, i_hbm, o_hbm, idx_v, rows_v):
        wid, n_mine = _sc_worker_chunks(nchunk)

        @pl.loop(0, n_mine)
        def _(j):
            c = wid + j * SC_WORKERS
            pltpu.sync_copy(i_hbm.at[c], idx_v)
            pltpu.sync_copy(y_hbm.at[idx_v], rows_v)
            pltpu.sync_copy(rows_v, o_hbm.at[pl.ds(c * SC_CHUNK, SC_CHUNK)])

    return k(y, idx)


def routing_plan(route_a, route_b, counts, n_tok):
    ids = jnp.concatenate([route_a[:, TOP_K:3 * TOP_K], route_b[:, TOP_K:3 * TOP_K]], axis=0).astype(I32)
    top_i = ids[:, 0:TOP_K]
    rank = ids[:, TOP_K:2 * TOP_K]
    cnt = counts[0, :N_EXPERTS].astype(I32)
    padded = ((cnt + MOE_TM - 1) // MOE_TM) * MOE_TM
    pend = jnp.cumsum(padded)
    pstart = pend - padded
    hot = top_i[:, :, None] == jnp.arange(N_EXPERTS, dtype=I32)[None, None, :]
    dest = rank + jnp.sum(jnp.where(hot, pstart[None, None, :], 0), axis=-1)
    nblk = -(-(n_tok * TOP_K) // MOE_TM) + N_EXPERTS
    n_used = pend[-1] // MOE_TM
    blk = jnp.arange(nblk, dtype=I32)
    blk_c = jnp.minimum(blk, n_used - 1)
    block_e = jnp.sum((blk_c[:, None] * MOE_TM >= pend[None, :]).astype(I32), axis=1)
    block_e = jnp.minimum(block_e, N_EXPERTS - 1)
    return dest.T, block_e, n_used.reshape(1).astype(I32), nblk * MOE_TM


def _alibi_slopes():
    return jnp.exp2(-8.0 * (jnp.arange(N_HEADS, dtype=F32) + 1.0) / N_HEADS)


def _row3(a):
    return a.reshape(a.shape[0], 1, a.shape[1])


def kernel(x_prompt, x_sample, cache_win0, cache_win1, cache_win2, state_conv, c_prompt, c_sample,
           w_mod, b_mod, w_in, b_in, w_oa, b_oa, conv_w, conv_b, conv_ln_g, conv_ln_b, w_pb, b_pb,
           w_out, b_out, ln1_g, ln1_b, w_router, b_router, w_up, b_up, w_down, b_down, ln2_g, ln2_b):
    depth = w_in.shape[0]
    b_, s, d = x_prompt.shape
    db, t_new, _ = x_sample.shape
    n_p, n_s = b_ * s, db * t_new
    n_tok = n_p + n_s
    tm_p = 512
    no_fold = (1,) * N_GROUPS
    slopes = _alibi_slopes()
    caches = [jnp.transpose(c, (0, 1, 3, 4, 5, 2)).reshape(c.shape[0], c.shape[1], 2 * LANE_W, c.shape[2])
              for c in (cache_win0, cache_win1, cache_win2)]
    wkv = w_in[:, :, ATT_W:3 * ATT_W].reshape(depth, D_MODEL, 2, N_GROUPS, LANE_W)
    wkv_t = jnp.transpose(wkv, (0, 3, 2, 4, 1)).reshape(depth, N_GROUPS, 2 * LANE_W, D_MODEL).astype(BF16)
    bkv = jnp.transpose(b_in[:, ATT_W:3 * ATT_W].reshape(depth, 2, N_GROUPS, LANE_W), (0, 2, 1, 3))
    bkv_t = jnp.broadcast_to(bkv.reshape(depth, N_GROUPS, 2 * LANE_W, 1), (depth, N_GROUPS, 2 * LANE_W, SPAD))

    mod_all = adaln_mod(jnp.concatenate([c_prompt, c_sample], axis=0), w_mod, b_mod)
    w_in_bf = w_in.astype(BF16)
    b_in3 = _row3(b_in)
    wr_pad = jnp.pad(w_router, ((0, 0), (0, 0), (0, ROUTER_LANES - N_EXPERTS)))
    wr_hi = wr_pad.astype(BF16)
    wr_lo = (wr_pad - wr_hi.astype(F32)).astype(BF16)
    br_pad = jnp.pad(b_router, ((0, 0), (0, ROUTER_LANES - N_EXPERTS)), constant_values=NEG_INF)
    mix_wts = (w_oa.astype(BF16), _row3(b_oa), w_pb.astype(BF16), _row3(b_pb), w_out.astype(BF16), _row3(b_out),
               _row3(ln1_g), _row3(ln1_b), wr_hi, wr_lo, _row3(br_pad))
    conv_b3, cg3, cb3 = _row3(conv_b), _row3(conv_ln_g), _row3(conv_ln_b)
    conv_w = jnp.repeat(conv_w, VREG_SUBLANES, axis=1)
    ln2g3, ln2b3 = _row3(ln2_g), _row3(ln2_b)
    e = w_up.shape[1]
    b_up4 = b_up.reshape(depth, e, 1, 2 * D_FF)
    b_down4 = b_down.reshape(depth, e, 1, D_MODEL)

    xp = x_prompt
    xs_ = x_sample.reshape(1, n_s, d)
    wp = [[] for _ in range(N_GROUPS)]
    ws = [jnp.zeros(c.shape, F32) for c in caches]
    cp, cs = [], []
    for l in range(depth):
        mod_p = mod_all[l, :b_].reshape(b_, 1, 6 * d)
        mod_s = jnp.repeat(mod_all[l, b_:], t_new, axis=0).reshape(1, n_s, 6 * d)

        (pq0, pq1, pq2, pkv0, pkv1, pkv2, pu, pga, pgb, pt0, pt1, pt2) = in_proj(
            xp, mod_p, w_in_bf, b_in3, l, tm_p, DILATIONS, wkv_t, bkv_t, WINDOWS)
        pkv = (pkv0, pkv1, pkv2)
        po, pl_ = [], []
        for g, qg in enumerate((pq0, pq1, pq2)):
            o, lse = attn_prompt(qg, pkv[g], slopes[g * LANES:(g + 1) * LANES], DILATIONS[g])
            po.append(o)
            pl_.append(lse)
            wp[g].append((pt0, pt1, pt2)[g])
        pcb = conv_branch(pu, pu, conv_w, conv_b3, cg3, cb3, l, tm_p, True)
        cp.append(pu[:, s - (CONV_WIDTH - 1):])

        sq0, sq1, sq2, skv0, skv1, skv2, su, sga, sgb = in_proj(xs_, mod_s, w_in_bf, b_in3, l, n_s, no_fold)
        del sq0, sq1, sq2, skv0, skv1, skv2
        x8 = jnp.pad(xs_.reshape(db, t_new, d), ((0, 0), (SQ - t_new, 0), (0, 0)))
        mod_seq = mod_all[l, b_:].reshape(db, 1, 6 * d)
        so, sl_ = [], []
        for g in range(N_GROUPS):
            o, lse, ws[g] = attn_sample(x8, mod_seq, w_in_bf, b_in3, wkv_t, bkv_t, caches[g], ws[g], l, g,
                                        slopes[g * LANES:(g + 1) * LANES], DILATIONS[g], t_new)
            so.append(o[:, SQ - t_new:].reshape(1, n_s, LANE_W))
            sl_.append(lse[:, SQ - t_new:].reshape(1, n_s, LANE_W))
        su3 = su.reshape(db, t_new, CONV_CH)
        full_s = jnp.concatenate([state_conv[l], su3], axis=1)
        cs.append(full_s[:, t_new:])
        st_pad = jnp.pad(state_conv[l], ((0, 0), (CONV_HALO - (CONV_WIDTH - 1), 0), (0, 0)))
        su8 = jnp.pad(su3, ((0, 0), (0, SQ - t_new), (0, 0)))
        scb = conv_branch(su8, st_pad, conv_w, conv_b3, cg3, cb3, l, SQ, False)
        scb = scb[:, :t_new].reshape(1, n_s, CONV_CH)

        px1, phu, plg = mix(po, pl_, pcb, pga, pgb, xp, mod_p, mix_wts, l, tm_p, DILATIONS)
        sx1, shu, slg = mix(so, sl_, scb, sga, sgb, xs_, mod_s, mix_wts, l, n_s, no_fold)

        route_p, counts_p = router(plg.reshape(n_p, ROUTER_LANES), jnp.zeros((1, ROUTER_LANES), F32))
        route_s, counts = router(slg.reshape(n_s, ROUTER_LANES), counts_p)
        dest4, block_e, n_used, cap = routing_plan(route_p, route_s, counts, n_tok)
        xsort = dispatch_rows(phu.reshape(n_p, ROW_SUB, ROW_LANE), shu.reshape(n_s, ROW_SUB, ROW_LANE), dest4, cap)
        ysort = moe_experts(xsort.reshape(cap * ROW_SUB, ROW_LANE), block_e, n_used,
                            w_up, b_up4, w_down, b_down4, l)
        n_pad = -(-n_tok // tm_p) * tm_p
        filler = (jnp.arange(TOP_K * (n_pad - n_tok), dtype=I32) * 997) % cap
        dest_pad = jnp.concatenate([dest4, filler.reshape(TOP_K, n_pad - n_tok)], axis=1)
        yg = gather_rows(ysort.reshape(cap, ROW_SUB, ROW_LANE), dest_pad)
        yg = yg.reshape(TOP_K * n_pad * ROW_SUB, ROW_LANE)

        xp = final(yg, route_p, px1, mod_p, ln2g3, ln2b3, l, tm_p, 0)
        xs_ = final(yg, route_s, sx1, mod_s, ln2g3, ln2b3, l, n_s, n_p)

    stack = lambda xs: jnp.stack(xs)
    to_win = lambda w: jnp.transpose(w.reshape(depth, w.shape[1], 2, LANES, HEAD_DIM, w.shape[-1]), (0, 1, 5, 2, 3, 4))
    return (xp, xs_.reshape(db, t_new, d),
            to_win(stack(wp[0])), to_win(stack(wp[1])), to_win(stack(wp[2])), stack(cp),
            to_win(ws[0]), to_win(ws[1]), to_win(ws[2]), stack(cs))
```

```python
import functools

import jax
import jax.numpy as jnp
from jax import lax
from jax.experimental import pallas as pl
from jax.experimental.pallas import tpu as pltpu
from jax.experimental.pallas import tpu_sc as plsc

F32 = jnp.float32
BF16 = jnp.bfloat16
U32 = jnp.uint32
I32 = jnp.int32

D_MODEL = 1024
N_GROUPS = 3
LANES = 4
HEAD_DIM = 64
LANE_W = LANES * HEAD_DIM
ATT_W = N_GROUPS * LANE_W
N_HEADS = N_GROUPS * LANES
WINDOWS = (128, 512, 2048)
DILATIONS = (1, 4, 16)
BAND = 128
CONV_CH = 512
CONV_WIDTH = 31
CONV_HALO = 32
N_EXPERTS = 32
TOP_K = 4
D_FF = 1024
SWIGLU_LIMIT = 7.0
SWIGLU_ALPHA = 1.702
N_IN = 3 * ATT_W + 2 * CONV_CH + 2 * D_MODEL
DEPTH_FOR_ALPHA = 4
ALPHA_DN = (2 * DEPTH_FOR_ALPHA) ** 0.25
LN_EPS = 1e-5
NEG_INF = -1e30

VREG_LANES = 128
VREG_SUBLANES = 8
QBLK = 128
ROUTER_LANES = VREG_LANES
MOE_TM = 512
PACK_W = D_MODEL // 2
ROW_LANE = VREG_LANES
ROW_SUB = PACK_W // ROW_LANE
VMEM_LIMIT = 56 * 1024 * 1024

_NT = (((1,), (1,)), ((), ()))


def _cparams(*sem):
    return pltpu.CompilerParams(dimension_semantics=sem, vmem_limit_bytes=VMEM_LIMIT)


def _sigmoid(x):
    return 1.0 / (1.0 + jnp.exp(-x))


def _layer_norm(r, g, b):
    mu = jnp.mean(r, axis=-1, keepdims=True)
    d = r - mu
    var = jnp.mean(d * d, axis=-1, keepdims=True)
    return d * lax.rsqrt(var + LN_EPS) * g + b


def _pack_bf16_pair(lo, hi):
    lo_b = lax.bitcast_convert_type(lo.astype(BF16).astype(F32), U32)
    hi_b = lax.bitcast_convert_type(hi.astype(BF16).astype(F32), U32)
    return (lo_b >> 16) | (hi_b & jnp.uint32(0xFFFF0000))


def _pack_bf16_pair_native(lo, hi):
    return pltpu.pack_elementwise([lo, hi], packed_dtype=BF16)


def _unpack_bf16_pair(w):
    lo = lax.bitcast_convert_type(w << 16, F32)
    hi = lax.bitcast_convert_type(w & jnp.uint32(0xFFFF0000), F32)
    return lo, hi


def _store_row_words(ref, words):
    rows = words.shape[0]
    for j in range(ROW_SUB):
        ref[pl.ds(j, rows, stride=ROW_SUB), :] = words[:, j * ROW_LANE:(j + 1) * ROW_LANE]


def _load_row_words(ref):
    rows = ref.shape[0] // ROW_SUB
    return jnp.concatenate([ref[pl.ds(j, rows, stride=ROW_SUB), :] for j in range(ROW_SUB)], axis=1)


def _mod_spec(mod, tm, width, col_block):
    if mod.shape[1] == 1:
        return pl.BlockSpec((1, 1, width), lambda g, i: (g, 0, col_block))
    return pl.BlockSpec((1, tm, width), lambda g, i: (g, i, col_block))


def _layer_spec(a, layer):
    return pl.BlockSpec((1,) + a.shape[1:], lambda g, i: (layer,) + (0,) * (a.ndim - 1))


def _mod_kernel(c_ref, w_ref, b_ref, o_ref):
    c = c_ref[...]
    a = (c * _sigmoid(c)).astype(BF16)
    o_ref[0] = jnp.dot(a, w_ref[0].astype(BF16), preferred_element_type=F32) + b_ref[0]


def adaln_mod(c_all, w_mod, b_mod):
    depth, d, n = w_mod.shape
    m = c_all.shape[0]
    tn = 1536
    return pl.pallas_call(
        _mod_kernel,
        grid=(depth, n // tn),
        in_specs=[pl.BlockSpec((m, d), lambda l, j: (0, 0)),
                  pl.BlockSpec((1, d, tn), lambda l, j: (l, 0, j)),
                  pl.BlockSpec((1, 1, tn), lambda l, j: (l, 0, j))],
        out_specs=pl.BlockSpec((1, m, tn), lambda l, j: (l, 0, j)),
        out_shape=jax.ShapeDtypeStruct((depth, m, n), F32),
        compiler_params=_cparams("arbitrary", "arbitrary"),
        name="adaln_mod",
    )(c_all, w_mod, b_mod.reshape(depth, 1, n))


def _conv_ln_swish(sh_ref, prev, cur, w_ref, cb, g, b, o_ref):
    for step in _conv_steps(sh_ref, prev, cur, w_ref, cb, g, b, o_ref):
        step()


def _conv_steps(sh_ref, prev, cur, w_ref, cb, g, b, o_ref):
    t_rows = cur.shape[0]
    n = CONV_HALO + t_rows
    rows = min(CONV_ROWS, t_rows)
    off = CONV_HALO - (CONV_WIDTH - 1)
    ngrp = rows // VREG_SUBLANES

    def setup():
        sh_ref[0, 0:CONV_HALO, :] = prev
        sh_ref[0, CONV_HALO:n, :] = cur
        for s in range(1, VREG_SUBLANES):
            sh_ref[s, 0:n - VREG_SUBLANES, :] = sh_ref[0, pl.ds(s, n - VREG_SUBLANES), :]

    def block(r0):
        accs = [None] * ngrp
        for w in range(CONV_WIDTH):
            s = (w + off) % VREG_SUBLANES
            a = r0 + (w + off) - s
            w8 = w_ref[0, w * VREG_SUBLANES:(w + 1) * VREG_SUBLANES, :]
            for k in range(ngrp):
                lo = a + k * VREG_SUBLANES
                term = sh_ref[s, lo:lo + VREG_SUBLANES, :] * w8
                accs[k] = term if accs[k] is None else accs[k] + term
        acc = jnp.concatenate(accs, axis=0) if ngrp > 1 else accs[0]
        z = _layer_norm(acc + cb, g, b)
        o_ref[0, r0:r0 + rows, :] = (z * _sigmoid(z)).astype(BF16)

    return [setup] + [functools.partial(block, r0) for r0 in range(0, t_rows, rows)]


def _in_proj_kernel(*refs, dils, windows):
    if windows is None:
        (x_ref, mod_ref, w_ref, b_ref, q0, q1, q2, kv0, kv1, kv2, u_ref, ga_ref, gb_ref, zs_ref) = refs
    else:
        (x_ref, mod_ref, w_ref, b_ref, wkv_ref, bkv_ref, cw_ref, cb_ref, cg_ref, cbeta_ref,
         q0, q1, q2, kv0, kv1, kv2, act_ref, ut_ref, ga_ref, gb_ref, t0, t1, t2,
         zs_ref, sh_ref, hist_ref) = refs
    x = x_ref[0]
    tm = x.shape[0]
    sh = mod_ref[0, :, 0:D_MODEL]
    sc = mod_ref[0, :, D_MODEL:2 * D_MODEL]
    h = (x * (1.0 + sc) + sh).astype(BF16)

    if windows is not None:
        i = pl.program_id(1)
        n_tiles = pl.num_programs(1)
        for g, t_ref in enumerate((t0, t1, t2)):
            @pl.when(i >= n_tiles - max(windows[g] // tm, 1))
            def _(g=g, t_ref=t_ref):
                bias = jnp.concatenate([bkv_ref[0, g]] * (tm // SPAD), axis=1)
                kvt = lax.dot_general(wkv_ref[0, g], h, _NT, preferred_element_type=F32) + bias
                t_ref[0] = kvt if windows[g] >= tm else kvt[:, tm - windows[g]:]

    def proj(c0, c1):
        return jnp.dot(h, w_ref[0, :, c0:c1], preferred_element_type=F32) + b_ref[0, :, c0:c1]

    def put(out_ref, col0, z, dil):
        if dil == 1:
            out_ref[0, :, col0:col0 + LANE_W] = z.astype(out_ref.dtype)
            return
        nt = LANE_W // VREG_LANES
        for c in range(nt):
            zs_ref[c] = z[:, c * VREG_LANES:(c + 1) * VREG_LANES]
        blk = out_ref.shape[2] // dil
        for r in range(dil):
            for c in range(nt):
                piece = zs_ref[c, pl.ds(r, tm // dil, stride=dil), :]
                lo = r * blk + col0 + c * VREG_LANES
                out_ref[0, :, lo:lo + VREG_LANES] = piece.astype(out_ref.dtype)

    o1 = 3 * ATT_W
    o2 = o1 + 2 * CONV_CH
    glu_a = proj(o1, o1 + CONV_CH)
    glu_b = proj(o1 + CONV_CH, o1 + 2 * CONV_CH)
    u = glu_a * _sigmoid(glu_b)
    conv_steps = []
    if windows is None:
        u_ref[0] = u
    else:
        prev = jnp.where(pl.program_id(1) == 0, 0.0, hist_ref[...])
        conv_steps = _conv_steps(sh_ref, prev, u, cw_ref, cb_ref[0], cg_ref[0], cbeta_ref[0], act_ref)
        conv_steps.pop(0)()
        hist_ref[...] = u[tm - CONV_HALO:tm]
        ut_ref[0] = u[tm - CONV_HALO:tm]

    jobs = []
    for g, (q_ref, kv_ref) in enumerate(((q0, kv0), (q1, kv1), (q2, kv2))):
        c = g * LANE_W
        jobs.append(functools.partial(
            lambda q_ref, c, g: put(q_ref, 0, proj(c, c + LANE_W) * (HEAD_DIM ** -0.5), dils[g]), q_ref, c, g))
        jobs.append(functools.partial(
            lambda kv_ref, c, g: put(kv_ref, 0, proj(ATT_W + c, ATT_W + c + LANE_W), dils[g]), kv_ref, c, g))
        jobs.append(functools.partial(
            lambda kv_ref, c, g: put(kv_ref, LANE_W, proj(2 * ATT_W + c, 2 * ATT_W + c + LANE_W), dils[g]),
            kv_ref, c, g))
    half = D_MODEL // 2
    for gate_ref, c0 in ((ga_ref, o2), (gb_ref, o2 + D_MODEL)):
        for hc in (0, half):
            def gate_job(gate_ref=gate_ref, c0=c0, hc=hc):
                gate_ref[0, :, hc:hc + half] = _sigmoid(proj(c0 + hc, c0 + hc + half)).astype(BF16)
            jobs.append(gate_job)
    per_job = -(-len(conv_steps) // len(jobs))
    for job in jobs:
        job()
        for _ in range(min(per_job, len(conv_steps))):
            conv_steps.pop(0)()


def in_proj(x, mod, w_bf, b, layer, tm, dils, wkv_t=None, bkv_t=None, windows=None, conv=None):
    g_, r, d = x.shape
    n_tiles = r // tm
    row = lambda w: pl.BlockSpec((1, tm, w), lambda g, i: (g, i, 0))
    fold = lambda w, dil: pl.BlockSpec((1, tm // dil, dil * w), lambda g, i: (g, i, 0))
    sds = lambda w, dt: jax.ShapeDtypeStruct((g_, r, w), dt)
    fsds = lambda w, dil, dt: jax.ShapeDtypeStruct((g_, r // dil, dil * w), dt)
    const = lambda a: pl.BlockSpec((1,) + a.shape[1:], lambda g, i: (layer,) + (0,) * (a.ndim - 1),
                                   pipeline_mode=pl.Buffered(1))
    in_specs = [row(d), _mod_spec(mod, tm, 2 * d, 0), const(w_bf), _layer_spec(b, layer)]
    args = [x, mod, w_bf, b]
    qkv_specs = [fold(LANE_W, dl) for dl in dils] + [fold(2 * LANE_W, dl) for dl in dils]
    qkv_shape = [fsds(LANE_W, dl, BF16) for dl in dils] + [fsds(2 * LANE_W, dl, F32) for dl in dils]
    scratch = [pltpu.VMEM((LANE_W // VREG_LANES, tm, VREG_LANES), F32)]
    if windows is None:
        out_specs = qkv_specs + [row(CONV_CH), row(d), row(d)]
        out_shape = qkv_shape + [sds(CONV_CH, F32), sds(d, BF16), sds(d, BF16)]
    else:
        in_specs += [const(wkv_t), _layer_spec(bkv_t, layer)] + [_layer_spec(a, layer) for a in conv]
        args += [wkv_t, bkv_t, *conv]
        out_specs = qkv_specs + [row(CONV_CH), pl.BlockSpec((1, CONV_HALO, CONV_CH), lambda g, i: (g, 0, 0)),
                                 row(d), row(d)]
        out_shape = qkv_shape + [sds(CONV_CH, BF16), jax.ShapeDtypeStruct((g_, CONV_HALO, CONV_CH), F32),
                                 sds(d, BF16), sds(d, BF16)]
        scratch += [pltpu.VMEM((VREG_SUBLANES, CONV_HALO + tm, CONV_CH), F32),
                    pltpu.VMEM((CONV_HALO, CONV_CH), F32)]
        for w in windows:
            assert w % tm == 0 or tm % w == 0
            first = n_tiles - max(w // tm, 1)
            out_specs.append(pl.BlockSpec((1, 2 * LANE_W, min(w, tm)),
                                          lambda g, i, first=first: (g, 0, jnp.maximum(i - first, 0))))
            out_shape.append(jax.ShapeDtypeStruct((g_, 2 * LANE_W, w), F32))
    return pl.pallas_call(
        functools.partial(_in_proj_kernel, dils=dils, windows=windows),
        grid=(g_, n_tiles),
        in_specs=in_specs,
        out_specs=out_specs,
        out_shape=out_shape,
        scratch_shapes=scratch,
        compiler_params=_cparams("arbitrary", "arbitrary"),
        name="in_proj",
    )(*args)


def _attn_kernel(q_ref, kvp_ref, kvc_ref, bias_ref, o_ref, l_ref, *, qb):
    i = pl.program_id(2)
    head_of_lane = lax.broadcasted_iota(I32, (QBLK, LANE_W), 1) >> 6
    ones = jnp.ones((2 * QBLK, VREG_LANES), BF16)
    first_cols = lax.broadcasted_iota(I32, (LANES * QBLK, 2 * QBLK), 1) < QBLK
    for j in range(qb):
        rs = slice(j * QBLK, (j + 1) * QBLK)
        q = q_ref[0, rs, :]
        q4 = jnp.concatenate([jnp.where(head_of_lane == h, q, jnp.zeros_like(q)) for h in range(LANES)], axis=0)
        if j == 0:
            kv = jnp.concatenate([kvp_ref[0], kvc_ref[0, rs, :]], axis=0)
        else:
            kv = kvc_ref[0, (j - 1) * QBLK:(j + 1) * QBLK, :]
        k2, v2 = kv[:, 0:LANE_W].astype(BF16), kv[:, LANE_W:2 * LANE_W].astype(BF16)
        s = lax.dot_general(q4, k2, _NT, preferred_element_type=F32) + bias_ref[...]
        if j == 0:
            s = jnp.where(jnp.logical_and(first_cols, i == 0), NEG_INF, s)
        m = jnp.max(s, axis=-1, keepdims=True)
        p = jnp.exp(s - m).astype(BF16)
        l = jnp.dot(p, ones, preferred_element_type=F32)
        o4 = jnp.dot(p, v2, preferred_element_type=F32)
        inv = 1.0 / l
        o4 = o4 * jnp.concatenate([inv, inv], axis=1)
        lse = m + jnp.log(l)
        lse2 = jnp.concatenate([lse, lse], axis=1)
        o, ls = o4[0:QBLK], lse2[0:QBLK]
        for h in range(1, LANES):
            sel = head_of_lane == h
            o = jnp.where(sel, o4[h * QBLK:(h + 1) * QBLK], o)
            ls = jnp.where(sel, lse2[h * QBLK:(h + 1) * QBLK], ls)
        o_ref[0, rs, :] = o.astype(BF16)
        l_ref[0, rs, :] = ls


def _band_bias(slopes_g, dil):
    qi = jnp.arange(QBLK)[:, None]
    kj = jnp.arange(QBLK)[None, :]
    dist_a = qi + BAND - kj
    dist_b = qi - kj
    sl = -(slopes_g * dil)[:, None, None]
    ba = jnp.where((dist_a <= BAND)[None], sl * dist_a.astype(F32)[None], NEG_INF)
    bb = jnp.where((dist_b >= 0)[None], sl * dist_b.astype(F32)[None], NEG_INF)
    return jnp.concatenate([ba, bb], axis=2).astype(F32).reshape(LANES * QBLK, 2 * QBLK)


def attn_prompt(qf, kvf, slopes_g, dil):
    b_, l_, _ = qf.shape
    qb = 4 if l_ % (4 * QBLK) == 0 else (2 if l_ % (2 * QBLK) == 0 else 1)
    nb = l_ // (qb * QBLK)
    bias = _band_bias(slopes_g, dil)
    bias_spec = pl.BlockSpec((LANES * QBLK, 2 * QBLK), lambda b, r, i: (0, 0))
    return pl.pallas_call(
        functools.partial(_attn_kernel, qb=qb),
        grid=(b_, dil, nb),
        in_specs=[pl.BlockSpec((1, qb * QBLK, LANE_W), lambda b, r, i: (b, i, r)),
                  pl.BlockSpec((1, QBLK, 2 * LANE_W), lambda b, r, i: (b, jnp.maximum(i * qb - 1, 0), r)),
                  pl.BlockSpec((1, qb * QBLK, 2 * LANE_W), lambda b, r, i: (b, i, r)),
                  bias_spec],
        out_specs=[pl.BlockSpec((1, qb * QBLK, LANE_W), lambda b, r, i: (b, i, r))] * 2,
        out_shape=[jax.ShapeDtypeStruct((b_, l_, dil * LANE_W), BF16),
                   jax.ShapeDtypeStruct((b_, l_, dil * LANE_W), F32)],
        compiler_params=_cparams("arbitrary", "arbitrary", "arbitrary"),
        name="attn_prompt",
    )(qf, kvf, kvf, bias)


SQ = 8
SPAD = 128


def _attn_s_kernel(x_ref, mod_ref, wq_ref, bq_ref, wkv_ref, bkv_ref, cache_ref, bias_ref, _tails_in,
                   o_ref, l_ref, tail_ref, hpad_ref, *, lw, t_new):
    nq = LANES * SQ
    blk = 2 * SQ

    @pl.when(pl.program_id(0) == 0)
    def _():
        hpad_ref[0:SPAD - blk, :] = jnp.zeros((SPAD - blk, D_MODEL), BF16)

    sh = mod_ref[0, :, 0:D_MODEL]
    sc = mod_ref[0, :, D_MODEL:2 * D_MODEL]
    h8 = x_ref[0] * (1.0 + sc) + sh
    h16 = jnp.concatenate([jnp.zeros_like(h8), h8], axis=0).astype(BF16)
    hpad_ref[SPAD - blk:SPAD, :] = h16
    q16 = (jnp.dot(h16, wq_ref[0], preferred_element_type=F32) + bq_ref[0]) * (HEAD_DIM ** -0.5)
    q8 = q16[SQ:blk].astype(BF16)
    head_of_lane = lax.broadcasted_iota(I32, (SQ, LANE_W), 1) >> 6
    q4 = jnp.concatenate([jnp.where(head_of_lane == h, q8, jnp.zeros_like(q8)) for h in range(LANES)], axis=0)
    kvn = lax.dot_general(wkv_ref[0, 0], hpad_ref[...], _NT, preferred_element_type=F32) + bkv_ref[0, 0]
    kc = cache_ref[0, 0, 0:LANE_W, :].astype(BF16)
    vc = cache_ref[0, 0, LANE_W:2 * LANE_W, :].astype(BF16)
    s_c = jnp.dot(q4, kc, preferred_element_type=F32) + bias_ref[:, 0:lw]
    s_n = jnp.dot(q4, kvn[0:LANE_W].astype(BF16), preferred_element_type=F32) + bias_ref[:, lw:lw + SPAD]
    m = jnp.maximum(jnp.max(s_c, axis=-1, keepdims=True), jnp.max(s_n, axis=-1, keepdims=True))
    p_c = jnp.exp(s_c - m)
    p_n = jnp.exp(s_n - m)
    l = jnp.sum(p_c, axis=-1, keepdims=True) + jnp.sum(p_n, axis=-1, keepdims=True)
    o4 = (lax.dot_general(p_c.astype(BF16), vc, _NT, preferred_element_type=F32)
          + lax.dot_general(p_n.astype(BF16), kvn[LANE_W:2 * LANE_W].astype(BF16), _NT,
                            preferred_element_type=F32)) / l
    lf = jnp.broadcast_to(m + jnp.log(l), (nq, LANE_W))
    rows = lax.broadcasted_iota(I32, (nq, LANE_W), 0)
    lanes = lax.broadcasted_iota(I32, (nq, LANE_W), 1)
    own = (rows >> 3) == (lanes >> 6)
    o4 = jnp.where(own, o4, 0.0)
    lf = jnp.where(own, lf, 0.0)
    o_ref[0] = (o4[0:SQ] + o4[SQ:2 * SQ] + o4[2 * SQ:3 * SQ] + o4[3 * SQ:4 * SQ]).astype(BF16)
    l_ref[0] = lf[0:SQ] + lf[SQ:2 * SQ] + lf[2 * SQ:3 * SQ] + lf[3 * SQ:4 * SQ]
    rolled = pltpu.roll(cache_ref[0, 0], lw - t_new, 1)
    if lw > SPAD:
        tail_ref[0, 0, :, 0:lw - SPAD] = rolled[:, 0:lw - SPAD]
    lane = lax.broadcasted_iota(I32, (2 * LANE_W, SPAD), 1)
    tail_ref[0, 0, :, lw - SPAD:lw] = jnp.where(lane >= SPAD - t_new, kvn, rolled[:, lw - SPAD:lw])


def _sample_bias(slopes_g, dil, lw, t_new):
    t = jnp.maximum(jnp.arange(SQ) - (SQ - t_new), 0)[:, None]
    r = jnp.arange(lw + SPAD)[None, :]
    pos = jnp.where(r < lw, r, r - (SPAD - t_new))
    dist = lw + t - pos
    jn = dist // dil
    real = (r < lw) | (r >= lw + SPAD - t_new)
    valid = real & (dist >= 0) & (dist % dil == 0) & (jn <= BAND)
    sl = (slopes_g * dil)[:, None, None]
    bias = jnp.where(valid[None], -sl * jn.astype(F32)[None], NEG_INF)
    return bias.reshape(LANES * SQ, lw + SPAD).astype(F32)


def attn_sample(x8, mod, w_bf, b3, wkv_t, bkv_t, cache_t, tails, layer, g, slopes_g, dil, t_new):
    db = x8.shape[0]
    depth, _, ch, lw = cache_t.shape
    bias = _sample_bias(slopes_g, dil, lw, t_new)
    in_specs = [pl.BlockSpec((1, SQ, D_MODEL), lambda b: (b, 0, 0)),
                pl.BlockSpec((1, 1, 2 * D_MODEL), lambda b: (b, 0, 0)),
                pl.BlockSpec((1, D_MODEL, LANE_W), lambda b: (layer, 0, g)),
                pl.BlockSpec((1, 1, LANE_W), lambda b: (layer, 0, g)),
                pl.BlockSpec((1, 1, ch, D_MODEL), lambda b: (layer, g, 0, 0)),
                pl.BlockSpec((1, 1, ch, SPAD), lambda b: (layer, g, 0, 0)),
                pl.BlockSpec((1, 1, ch, lw), lambda b: (layer, b, 0, 0)),
                pl.BlockSpec((LANES * SQ, lw + SPAD), lambda b: (0, 0)),
                pl.BlockSpec(memory_space=pl.ANY)]
    args = [x8, mod, w_bf, b3, wkv_t, bkv_t, cache_t, bias, tails]
    aliases = {len(args) - 1: 2}
    return pl.pallas_call(
        functools.partial(_attn_s_kernel, lw=lw, t_new=t_new),
        grid=(db,),
        in_specs=in_specs,
        out_specs=[pl.BlockSpec((1, SQ, LANE_W), lambda b: (b, 0, 0)),
                   pl.BlockSpec((1, SQ, LANE_W), lambda b: (b, 0, 0)),
                   pl.BlockSpec((1, 1, ch, lw), lambda b: (layer, b, 0, 0))],
        out_shape=[jax.ShapeDtypeStruct((db, SQ, LANE_W), BF16),
                   jax.ShapeDtypeStruct((db, SQ, LANE_W), F32),
                   jax.ShapeDtypeStruct((depth, db, ch, lw), F32)],
        scratch_shapes=[pltpu.VMEM((SPAD, D_MODEL), BF16)],
        input_output_aliases=aliases,
        compiler_params=_cparams("arbitrary"),
        name="attn_sample",
    )(*args)


CONV_ROWS = 16


def _conv_kernel(prev_ref, cur_ref, w_ref, cb_ref, g_ref, b_ref, o_ref, sh_ref, *, zero_first):
    prev = prev_ref[0]
    if zero_first:
        prev = jnp.where(pl.program_id(1) == 0, 0.0, prev)
    _conv_ln_swish(sh_ref, prev, cur_ref[0], w_ref, cb_ref[0], g_ref[0], b_ref[0], o_ref)


def conv_branch(u, prev_src, conv_w, conv_b, ln_g, ln_b, layer, tm, zero_first):
    g_, r, c = u.shape
    per = tm // CONV_HALO
    if zero_first:
        prev_map = lambda g, i: (g, jnp.maximum(i * per - 1, 0), 0)
    else:
        prev_map = lambda g, i: (g, 0, 0)
    return pl.pallas_call(
        functools.partial(_conv_kernel, zero_first=zero_first),
        grid=(g_, r // tm),
        in_specs=[pl.BlockSpec((1, CONV_HALO, c), prev_map),
                  pl.BlockSpec((1, tm, c), lambda g, i: (g, i, 0)),
                  _layer_spec(conv_w, layer), _layer_spec(conv_b, layer),
                  _layer_spec(ln_g, layer), _layer_spec(ln_b, layer)],
        out_specs=pl.BlockSpec((1, tm, c), lambda g, i: (g, i, 0)),
        out_shape=jax.ShapeDtypeStruct((g_, r, c), BF16),
        scratch_shapes=[pltpu.VMEM((VREG_SUBLANES, CONV_HALO + tm, c), F32)],
        compiler_params=_cparams("arbitrary", "arbitrary"),
        name="conv_branch",
    )(prev_src, u, conv_w, conv_b, ln_g, ln_b)


def _mix_kernel(o0, o1, o2, l0, l1, l2, cb_ref, ga_ref, gb_ref, x_ref, mod_ref,
                woa_ref, boa_ref, wpb_ref, bpb_ref, wout_ref, bout_ref, g1_ref, b1_ref,
                wrh_ref, wrl_ref, br_ref, tri_ref, cin_ref, x1_ref, hu_ref, route_ref, cnt_ref,
                s0, s1, s2, s3, carry_ref, *, dils):
    tm = x_ref.shape[1]

    @pl.when(jnp.logical_and(pl.program_id(0) == 0, pl.program_id(1) == 0))
    def _():
        carry_ref[...] = cin_ref[...]

    def unfold(ref, dil, scratch):
        if dil == 1:
            return ref[0].astype(F32)
        nt = LANE_W // VREG_LANES
        for r in range(dil):
            for c in range(nt):
                lo = r * LANE_W + c * VREG_LANES
                scratch[c, pl.ds(r, tm // dil, stride=dil), :] = ref[0, :, lo:lo + VREG_LANES].astype(F32)
        return jnp.concatenate([scratch[c] for c in range(nt)], axis=1)

    assert dils[0] == 1
    oa, ob, oc = unfold(o0, 1, None), unfold(o1, dils[1], s0), unfold(o2, dils[2], s1)
    la, lb, lc = unfold(l0, 1, None), unfold(l1, dils[1], s2), unfold(l2, dils[2], s3)
    lmax = jnp.maximum(jnp.maximum(la, lb), lc)
    ea, eb, ec = jnp.exp(la - lmax), jnp.exp(lb - lmax), jnp.exp(lc - lmax)
    merged = (ea * oa + eb * ob + ec * oc) / (ea + eb + ec)
    br_a = jnp.dot(merged.astype(BF16), woa_ref[0], preferred_element_type=F32) + boa_ref[0]
    br_b = jnp.dot(cb_ref[0], wpb_ref[0], preferred_element_type=F32) + bpb_ref[0]
    mixed = ga_ref[0].astype(F32) * br_a + gb_ref[0].astype(F32) * br_b
    y = jnp.dot(mixed.astype(BF16), wout_ref[0], preferred_element_type=F32) + bout_ref[0]
    gt1 = mod_ref[0, :, 2 * D_MODEL:3 * D_MODEL]
    sh2 = mod_ref[0, :, 3 * D_MODEL:4 * D_MODEL]
    sc2 = mod_ref[0, :, 4 * D_MODEL:5 * D_MODEL]
    x1 = _layer_norm(ALPHA_DN * x_ref[0] + (1.0 + gt1) * y, g1_ref[0], b1_ref[0])
    x1_ref[0] = x1
    h2 = x1 * (1.0 + sc2) + sh2
    hi = h2.astype(BF16)
    lo = (h2 - hi.astype(F32)).astype(BF16)
    logits = (jnp.dot(hi, wrh_ref[0], preferred_element_type=F32)
              + jnp.dot(lo, wrh_ref[0], preferred_element_type=F32)
              + jnp.dot(hi, wrl_ref[0], preferred_element_type=F32) + br_ref[0])
    _store_row_words(hu_ref.at[0], _pack_bf16_pair(h2[:, 0:PACK_W], h2[:, PACK_W:D_MODEL]))
    _route_tile(logits, tri_ref, carry_ref, route_ref, cnt_ref)


def mix(o_g, l_g, cb, ga, gb, x, mod, wts, counts_in, layer, tm, dils):
    g_, r, d = x.shape
    row = lambda w: pl.BlockSpec((1, tm, w), lambda g, i: (g, i, 0))
    fold = lambda dil: pl.BlockSpec((1, tm // dil, dil * LANE_W), lambda g, i: (g, i, 0))
    whole = lambda a: pl.BlockSpec(a.shape, lambda g, i: (0,) * a.ndim)
    tri = (jnp.arange(tm)[None, :] < jnp.arange(tm)[:, None]).astype(BF16)
    return pl.pallas_call(
        functools.partial(_mix_kernel, dils=dils),
        grid=(g_, r // tm),
        in_specs=[fold(dl) for dl in dils] * 2 + [row(CONV_CH), row(d), row(d), row(d), _mod_spec(mod, tm, 6 * d, 0)]
                 + [_layer_spec(a, layer) for a in wts] + [whole(tri), whole(counts_in)],
        out_specs=[row(d), pl.BlockSpec((1, tm * ROW_SUB, ROW_LANE), lambda g, i: (g, i, 0)), row(ROUTER_LANES),
                   pl.BlockSpec((1, ROUTER_LANES), lambda g, i: (0, 0))],
        out_shape=[jax.ShapeDtypeStruct((g_, r, d), F32),
                   jax.ShapeDtypeStruct((g_, r * ROW_SUB, ROW_LANE), U32),
                   jax.ShapeDtypeStruct((g_, r, ROUTER_LANES), F32),
                   jax.ShapeDtypeStruct((1, ROUTER_LANES), F32)],
        scratch_shapes=[pltpu.VMEM((LANE_W // VREG_LANES, tm, VREG_LANES), F32)] * 4
                       + [pltpu.VMEM((1, ROUTER_LANES), F32)],
        compiler_params=_cparams("arbitrary", "arbitrary"),
        name="mix",
    )(*o_g, *l_g, cb, ga, gb, x, mod, *wts, tri, counts_in)


def _route_tile(work, tri_ref, carry_ref, out_ref, cnt_ref):
    tm = work.shape[0]
    lane = lax.broadcasted_iota(I32, (tm, ROUTER_LANES), 1)
    vals, idxs, hots = [], [], []
    for _ in range(TOP_K):
        m = jnp.max(work, axis=-1, keepdims=True)
        idx = jnp.min(jnp.where(work == m, lane, ROUTER_LANES), axis=-1, keepdims=True)
        hot = lane == idx
        work = jnp.where(hot, -jnp.inf, work)
        vals.append(m)
        idxs.append(idx)
        hots.append(hot)
    es = [jnp.exp(v - vals[0]) for v in vals]
    den = es[0] + es[1] + es[2] + es[3]
    member = jnp.where(hots[0] | hots[1] | hots[2] | hots[3], 1.0, 0.0)
    before = jnp.dot(tri_ref[...], member.astype(BF16), preferred_element_type=F32) + carry_ref[...]
    out = jnp.zeros((tm, ROUTER_LANES), F32)
    for k in range(TOP_K):
        rank = jnp.sum(jnp.where(hots[k], before, 0.0), axis=-1, keepdims=True)
        out = jnp.where(lane == k, es[k] / den, out)
        out = jnp.where(lane == TOP_K + k, idxs[k].astype(F32), out)
        out = jnp.where(lane == 2 * TOP_K + k, rank, out)
    out_ref[0] = out
    carry_ref[...] = carry_ref[...] + jnp.sum(member, axis=0, keepdims=True)
    cnt_ref[...] = carry_ref[...]


def _moe_kernel(be_ref, nu_ref, xs_ref, wup_ref, bup_ref, wdn_ref, bdn_ref, y_ref, wup_bf, wdn_bf, act_ref):
    i = pl.program_id(0)

    @pl.when(i < nu_ref[0])
    def _():
        prev_e = be_ref[jnp.maximum(i - 1, 0)]

        @pl.when((i == 0) | (be_ref[i] != prev_e))
        def _():
            step = 128
            for r0 in range(0, D_MODEL, step):
                wup_bf[r0:r0 + step, :] = wup_ref[0, 0, r0:r0 + step, :].astype(BF16)
            for r0 in range(0, D_FF, step):
                wdn_bf[r0:r0 + step, :] = wdn_ref[0, 0, r0:r0 + step, :].astype(BF16)

        lo, hi = _unpack_bf16_pair(_load_row_words(xs_ref))
        x = jnp.concatenate([lo, hi], axis=1).astype(BF16)
        cw = 512
        for c0 in range(0, D_FF, cw):
            a = jnp.dot(x, wup_bf[:, c0:c0 + cw], preferred_element_type=F32) + bup_ref[0, 0, :, c0:c0 + cw]
            lin = (jnp.dot(x, wup_bf[:, D_FF + c0:D_FF + c0 + cw], preferred_element_type=F32)
                   + bup_ref[0, 0, :, D_FF + c0:D_FF + c0 + cw])
            a = jnp.minimum(a, SWIGLU_LIMIT)
            lin = jnp.clip(lin, -SWIGLU_LIMIT, SWIGLU_LIMIT)
            act_ref[:, c0:c0 + cw] = (a * _sigmoid(SWIGLU_ALPHA * a) * (lin + 1.0)).astype(BF16)
        y = jnp.dot(act_ref[...], wdn_bf[...], preferred_element_type=F32) + bdn_ref[0, 0]
        _store_row_words(y_ref, _pack_bf16_pair_native(y[:, 0:PACK_W], y[:, PACK_W:D_MODEL]))


def moe_experts(xs, block_e, n_used, w_up, b_up, w_down, b_down, layer):
    cap = xs.shape[0] // ROW_SUB
    nblk = cap // MOE_TM
    row_map = lambda i, be, nu: (jnp.minimum(i, nu[0] - 1), 0)
    exp_map = lambda i, be, nu: (layer, be[i], 0, 0)
    grid_spec = pltpu.PrefetchScalarGridSpec(
        num_scalar_prefetch=2,
        grid=(nblk,),
        in_specs=[pl.BlockSpec((MOE_TM * ROW_SUB, ROW_LANE), row_map),
                  pl.BlockSpec((1, 1, D_MODEL, 2 * D_FF), exp_map),
                  pl.BlockSpec((1, 1, 1, 2 * D_FF), exp_map),
                  pl.BlockSpec((1, 1, D_FF, D_MODEL), exp_map),
                  pl.BlockSpec((1, 1, 1, D_MODEL), exp_map)],
        out_specs=pl.BlockSpec((MOE_TM * ROW_SUB, ROW_LANE), row_map),
        scratch_shapes=[pltpu.VMEM((D_MODEL, 2 * D_FF), BF16),
                        pltpu.VMEM((D_FF, D_MODEL), BF16),
                        pltpu.VMEM((MOE_TM, D_FF), BF16)],
    )
    return pl.pallas_call(
        _moe_kernel,
        grid_spec=grid_spec,
        out_shape=jax.ShapeDtypeStruct((cap * ROW_SUB, ROW_LANE), U32),
        compiler_params=_cparams("arbitrary"),
        name="moe_experts",
    )(block_e, n_used, xs, w_up, b_up, w_down, b_down)


def _final_kernel(y0, y1, y2, y3, r_ref, x_ref, mod_ref, g2_ref, b2_ref, o_ref):
    gates = r_ref[...]
    acc_lo = None
    for k, y_ref in enumerate((y0, y1, y2, y3)):
        lo, hi = _unpack_bf16_pair(_load_row_words(y_ref))
        gk = gates[:, k:k + 1]
        acc_lo = gk * lo if acc_lo is None else acc_lo + gk * lo
        acc_hi = gk * hi if k == 0 else acc_hi + gk * hi
    y = jnp.concatenate([acc_lo, acc_hi], axis=1)
    gt2 = mod_ref[0]
    o_ref[0] = _layer_norm(ALPHA_DN * x_ref[0] + (1.0 + gt2) * y, g2_ref[0], b2_ref[0])


def final(yg, route, x1, mod, ln_g, ln_b, layer, tm, row0):
    g_, r, d = x1.shape
    per = r // tm
    base = row0 // tm
    nblk_k = yg.shape[0] // (TOP_K * tm * ROW_SUB)
    yspec = lambda k: pl.BlockSpec((tm * ROW_SUB, ROW_LANE), lambda g, i: (k * nblk_k + base + g * per + i, 0))
    return pl.pallas_call(
        _final_kernel,
        grid=(g_, per),
        in_specs=[yspec(0), yspec(1), yspec(2), yspec(3),
                  pl.BlockSpec((tm, ROUTER_LANES), lambda g, i: (g * per + i, 0)),
                  pl.BlockSpec((1, tm, d), lambda g, i: (g, i, 0)), _mod_spec(mod, tm, d, 5),
                  _layer_spec(ln_g, layer), _layer_spec(ln_b, layer)],
        out_specs=pl.BlockSpec((1, tm, d), lambda g, i: (g, i, 0)),
        out_shape=jax.ShapeDtypeStruct((g_, r, d), F32),
        compiler_params=_cparams("arbitrary", "arbitrary"),
        name="final",
    )(yg, yg, yg, yg, route, x1, mod, ln_g, ln_b)


SC_CORES = 2
SC_SUBCORES = 16
SC_WORKERS = SC_CORES * SC_SUBCORES
SC_CHUNK = 128


def _sc_mesh():
    return plsc.VectorSubcoreMesh(core_axis_name="c", subcore_axis_name="s")


def _sc_worker_chunks(nchunk):
    wid = lax.axis_index("s") * SC_CORES + lax.axis_index("c")
    return wid, (nchunk - wid + SC_WORKERS - 1) // SC_WORKERS


def dispatch_rows(hu_a, hu_b, dest4, cap):
    chunks_a = hu_a.shape[0] // SC_CHUNK
    nchunk = chunks_a + hu_b.shape[0] // SC_CHUNK
    idx = dest4.reshape(TOP_K, nchunk, SC_CHUNK)

    @functools.partial(
        pl.kernel, mesh=_sc_mesh(),
        out_type=jax.ShapeDtypeStruct((cap, ROW_SUB, ROW_LANE), U32),
        scratch_types=[pltpu.VMEM((SC_CHUNK,), I32)] * TOP_K
                      + [pltpu.VMEM((SC_CHUNK, ROW_SUB, ROW_LANE), U32)])
    def k(a_hbm, b_hbm, i_hbm, o_hbm, i0, i1, i2, i3, rows_v):
        wid, n_mine = _sc_worker_chunks(nchunk)

        @pl.loop(0, n_mine)
        def _(j):
            c = wid + j * SC_WORKERS

            @pl.when(c < chunks_a)
            def _():
                pltpu.sync_copy(a_hbm.at[pl.ds(c * SC_CHUNK, SC_CHUNK)], rows_v)

            @pl.when(c >= chunks_a)
            def _():
                pltpu.sync_copy(b_hbm.at[pl.ds((c - chunks_a) * SC_CHUNK, SC_CHUNK)], rows_v)

            for kk, iv in enumerate((i0, i1, i2, i3)):
                pltpu.sync_copy(i_hbm.at[kk, c], iv)
            for iv in (i0, i1, i2, i3):
                pltpu.sync_copy(rows_v, o_hbm.at[iv])

    return k(hu_a, hu_b, idx)


def gather_rows(y, dest4):
    m = dest4.shape[0] * dest4.shape[1]
    nchunk = m // SC_CHUNK
    idx = dest4.reshape(nchunk, SC_CHUNK)

    @functools.partial(
        pl.kernel, mesh=_sc_mesh(),
        out_type=jax.ShapeDtypeStruct((m, ROW_SUB, ROW_LANE), U32),
        scratch_types=[pltpu.VMEM((SC_CHUNK,), I32), pltpu.VMEM((SC_CHUNK, ROW_SUB, ROW_LANE), U32)])
    def k(y_hbm, i_hbm, o_hbm, idx_v, rows_v):
        wid, n_mine = _sc_worker_chunks(nchunk)

        @pl.loop(0, n_mine)
        def _(j):
            c = wid + j * SC_WORKERS
            pltpu.sync_copy(i_hbm.at[c], idx_v)
            pltpu.sync_copy(y_hbm.at[idx_v], rows_v)
            pltpu.sync_copy(rows_v, o_hbm.at[pl.ds(c * SC_CHUNK, SC_CHUNK)])

    return k(y, idx)


def routing_plan(route_a, route_b, counts, n_tok):
    ids = jnp.concatenate([route_a[:, TOP_K:3 * TOP_K], route_b[:, TOP_K:3 * TOP_K]], axis=0).astype(I32)
    top_i = ids[:, 0:TOP_K]
    rank = ids[:, TOP_K:2 * TOP_K]
    cnt = counts[0, :N_EXPERTS].astype(I32)
    padded = ((cnt + MOE_TM - 1) // MOE_TM) * MOE_TM
    pend = jnp.cumsum(padded)
    pstart = pend - padded
    hot = top_i[:, :, None] == jnp.arange(N_EXPERTS, dtype=I32)[None, None, :]
    dest = rank + jnp.sum(jnp.where(hot, pstart[None, None, :], 0), axis=-1)
    nblk = -(-(n_tok * TOP_K) // MOE_TM) + N_EXPERTS
    n_used = pend[-1] // MOE_TM
    blk = jnp.arange(nblk, dtype=I32)
    blk_c = jnp.minimum(blk, n_used - 1)
    block_e = jnp.sum((blk_c[:, None] * MOE_TM >= pend[None, :]).astype(I32), axis=1)
    block_e = jnp.minimum(block_e, N_EXPERTS - 1)
    return dest.T, block_e, n_used.reshape(1).astype(I32), nblk * MOE_TM


def _alibi_slopes():
    return jnp.exp2(-8.0 * (jnp.arange(N_HEADS, dtype=F32) + 1.0) / N_HEADS)


def _row3(a):
    return a.reshape(a.shape[0], 1, a.shape[1])


def kernel(x_prompt, x_sample, cache_win0, cache_win1, cache_win2, state_conv, c_prompt, c_sample,
           w_mod, b_mod, w_in, b_in, w_oa, b_oa, conv_w, conv_b, conv_ln_g, conv_ln_b, w_pb, b_pb,
           w_out, b_out, ln1_g, ln1_b, w_router, b_router, w_up, b_up, w_down, b_down, ln2_g, ln2_b):
    depth = w_in.shape[0]
    b_, s, d = x_prompt.shape
    db, t_new, _ = x_sample.shape
    n_p, n_s = b_ * s, db * t_new
    n_tok = n_p + n_s
    tm_p = 512
    no_fold = (1,) * N_GROUPS
    slopes = _alibi_slopes()
    caches = [jnp.transpose(c, (0, 1, 3, 4, 5, 2)).reshape(c.shape[0], c.shape[1], 2 * LANE_W, c.shape[2])
              for c in (cache_win0, cache_win1, cache_win2)]
    wkv = w_in[:, :, ATT_W:3 * ATT_W].reshape(depth, D_MODEL, 2, N_GROUPS, LANE_W)
    wkv_t = jnp.transpose(wkv, (0, 3, 2, 4, 1)).reshape(depth, N_GROUPS, 2 * LANE_W, D_MODEL).astype(BF16)
    bkv = jnp.transpose(b_in[:, ATT_W:3 * ATT_W].reshape(depth, 2, N_GROUPS, LANE_W), (0, 2, 1, 3))
    bkv_t = jnp.broadcast_to(bkv.reshape(depth, N_GROUPS, 2 * LANE_W, 1), (depth, N_GROUPS, 2 * LANE_W, SPAD))

    mod_all = adaln_mod(jnp.concatenate([c_prompt, c_sample], axis=0), w_mod, b_mod)
    w_in_bf = w_in.astype(BF16)
    b_in3 = _row3(b_in)
    wr_pad = jnp.pad(w_router, ((0, 0), (0, 0), (0, ROUTER_LANES - N_EXPERTS)))
    wr_hi = wr_pad.astype(BF16)
    wr_lo = (wr_pad - wr_hi.astype(F32)).astype(BF16)
    br_pad = jnp.pad(b_router, ((0, 0), (0, ROUTER_LANES - N_EXPERTS)), constant_values=NEG_INF)
    mix_wts = (w_oa.astype(BF16), _row3(b_oa), w_pb.astype(BF16), _row3(b_pb), w_out.astype(BF16), _row3(b_out),
               _row3(ln1_g), _row3(ln1_b), wr_hi, wr_lo, _row3(br_pad))
    conv_b3, cg3, cb3 = _row3(conv_b), _row3(conv_ln_g), _row3(conv_ln_b)
    conv_w = jnp.repeat(conv_w, VREG_SUBLANES, axis=1)
    ln2g3, ln2b3 = _row3(ln2_g), _row3(ln2_b)
    e = w_up.shape[1]
    b_up4 = b_up.reshape(depth, e, 1, 2 * D_FF)
    b_down4 = b_down.reshape(depth, e, 1, D_MODEL)

    xp = x_prompt
    xs_ = x_sample.reshape(1, n_s, d)
    wp = [[] for _ in range(N_GROUPS)]
    ws = [jnp.zeros(c.shape, F32) for c in caches]
    cp, cs = [], []
    for l in range(depth):
        mod_p = mod_all[l, :b_].reshape(b_, 1, 6 * d)
        mod_s = jnp.repeat(mod_all[l, b_:], t_new, axis=0).reshape(1, n_s, 6 * d)

        (pq0, pq1, pq2, pkv0, pkv1, pkv2, pcb, put, pga, pgb, pt0, pt1, pt2) = in_proj(
            xp, mod_p, w_in_bf, b_in3, l, tm_p, DILATIONS, wkv_t, bkv_t, WINDOWS,
            (conv_w, conv_b3, cg3, cb3))
        pkv = (pkv0, pkv1, pkv2)
        po, pl_ = [], []
        for g, qg in enumerate((pq0, pq1, pq2)):
            o, lse = attn_prompt(qg, pkv[g], slopes[g * LANES:(g + 1) * LANES], DILATIONS[g])
            po.append(o)
            pl_.append(lse)
            wp[g].append((pt0, pt1, pt2)[g])
        cp.append(put[:, CONV_HALO - (CONV_WIDTH - 1):])

        sq0, sq1, sq2, skv0, skv1, skv2, su, sga, sgb = in_proj(xs_, mod_s, w_in_bf, b_in3, l, n_s, no_fold)
        del sq0, sq1, sq2, skv0, skv1, skv2
        x8 = jnp.pad(xs_.reshape(db, t_new, d), ((0, 0), (SQ - t_new, 0), (0, 0)))
        mod_seq = mod_all[l, b_:].reshape(db, 1, 6 * d)
        so, sl_ = [], []
        for g in range(N_GROUPS):
            o, lse, ws[g] = attn_sample(x8, mod_seq, w_in_bf, b_in3, wkv_t, bkv_t, caches[g], ws[g], l, g,
                                        slopes[g * LANES:(g + 1) * LANES], DILATIONS[g], t_new)
            so.append(o[:, SQ - t_new:].reshape(1, n_s, LANE_W))
            sl_.append(lse[:, SQ - t_new:].reshape(1, n_s, LANE_W))
        su3 = su.reshape(db, t_new, CONV_CH)
        full_s = jnp.concatenate([state_conv[l], su3], axis=1)
        cs.append(full_s[:, t_new:])
        st_pad = jnp.pad(state_conv[l], ((0, 0), (CONV_HALO - (CONV_WIDTH - 1), 0), (0, 0)))
        su8 = jnp.pad(su3, ((0, 0), (0, SQ - t_new), (0, 0)))
        scb = conv_branch(su8, st_pad, conv_w, conv_b3, cg3, cb3, l, SQ, False)
        scb = scb[:, :t_new].reshape(1, n_s, CONV_CH)

        px1, phu, route_p, counts_p = mix(po, pl_, pcb, pga, pgb, xp, mod_p, mix_wts,
                                          jnp.zeros((1, ROUTER_LANES), F32), l, tm_p, DILATIONS)
        sx1, shu, route_s, counts = mix(so, sl_, scb, sga, sgb, xs_, mod_s, mix_wts, counts_p, l, n_s, no_fold)
        route_p = route_p.reshape(n_p, ROUTER_LANES)
        route_s = route_s.reshape(n_s, ROUTER_LANES)

        dest4, block_e, n_used, cap = routing_plan(route_p, route_s, counts, n_tok)
        xsort = dispatch_rows(phu.reshape(n_p, ROW_SUB, ROW_LANE), shu.reshape(n_s, ROW_SUB, ROW_LANE), dest4, cap)
        ysort = moe_experts(xsort.reshape(cap * ROW_SUB, ROW_LANE), block_e, n_used,
                            w_up, b_up4, w_down, b_down4, l)
        n_pad = -(-n_tok // tm_p) * tm_p
        filler = (jnp.arange(TOP_K * (n_pad - n_tok), dtype=I32) * 997) % cap
        dest_pad = jnp.concatenate([dest4, filler.reshape(TOP_K, n_pad - n_tok)], axis=1)
        yg = gather_rows(ysort.reshape(cap, ROW_SUB, ROW_LANE), dest_pad)
        yg = yg.reshape(TOP_K * n_pad * ROW_SUB, ROW_LANE)

        xp = final(yg, route_p, px1, mod_p, ln2g3, ln2b3, l, tm_p, 0)
        xs_ = final(yg, route_s, sx1, mod_s, ln2g3, ln2b3, l, n_s, n_p)

    stack = lambda xs: jnp.stack(xs)
    to_win = lambda w: jnp.transpose(w.reshape(depth, w.shape[1], 2, LANES, HEAD_DIM, w.shape[-1]), (0, 1, 5, 2, 3, 4))
    return (xp, xs_.reshape(db, t_new, d),
            to_win(stack(wp[0])), to_win(stack(wp[1])), to_win(stack(wp[2])), stack(cp),
            to_win(ws[0]), to_win(ws[1]), to_win(ws[2]), stack(cs))
```

```python
import functools

import jax
import jax.numpy as jnp
from jax import lax
from jax.experimental import pallas as pl
from jax.experimental.pallas import tpu as pltpu
from jax.experimental.pallas import tpu_sc as plsc

F32 = jnp.float32
BF16 = jnp.bfloat16
U32 = jnp.uint32
I32 = jnp.int32

D_MODEL = 1024
N_GROUPS = 3
LANES = 4
HEAD_DIM = 64
LANE_W = LANES * HEAD_DIM
ATT_W = N_GROUPS * LANE_W
N_HEADS = N_GROUPS * LANES
WINDOWS = (128, 512, 2048)
DILATIONS = (1, 4, 16)
BAND = 128
CONV_CH = 512
CONV_WIDTH = 31
CONV_HALO = 32
N_EXPERTS = 32
TOP_K = 4
D_FF = 1024
SWIGLU_LIMIT = 7.0
SWIGLU_ALPHA = 1.702
N_IN = 3 * ATT_W + 2 * CONV_CH + 2 * D_MODEL
DEPTH_FOR_ALPHA = 4
ALPHA_DN = (2 * DEPTH_FOR_ALPHA) ** 0.25
LN_EPS = 1e-5
NEG_INF = -1e30

VREG_LANES = 128
VREG_SUBLANES = 8
QBLK = 128
ROUTER_LANES = VREG_LANES
MOE_TM = 512
PACK_W = D_MODEL // 2
ROW_LANE = VREG_LANES
ROW_SUB = PACK_W // ROW_LANE
VMEM_LIMIT = 56 * 1024 * 1024

_NT = (((1,), (1,)), ((), ()))


def _cparams(*sem):
    return pltpu.CompilerParams(dimension_semantics=sem, vmem_limit_bytes=VMEM_LIMIT)


def _sigmoid(x):
    return 1.0 / (1.0 + jnp.exp(-x))


def _layer_norm(r, g, b):
    mu = jnp.mean(r, axis=-1, keepdims=True)
    d = r - mu
    var = jnp.mean(d * d, axis=-1, keepdims=True)
    return d * lax.rsqrt(var + LN_EPS) * g + b


def _pack_bf16_pair(lo, hi):
    lo_b = lax.bitcast_convert_type(lo.astype(BF16).astype(F32), U32)
    hi_b = lax.bitcast_convert_type(hi.astype(BF16).astype(F32), U32)
    return (lo_b >> 16) | (hi_b & jnp.uint32(0xFFFF0000))


def _pack_bf16_pair_native(lo, hi):
    return pltpu.pack_elementwise([lo, hi], packed_dtype=BF16)


def _unpack_bf16_pair(w):
    lo = lax.bitcast_convert_type(w << 16, F32)
    hi = lax.bitcast_convert_type(w & jnp.uint32(0xFFFF0000), F32)
    return lo, hi


def _store_row_words(ref, words):
    rows = words.shape[0]
    for j in range(ROW_SUB):
        ref[pl.ds(j, rows, stride=ROW_SUB), :] = words[:, j * ROW_LANE:(j + 1) * ROW_LANE]


def _load_row_words(ref):
    rows = ref.shape[0] // ROW_SUB
    return jnp.concatenate([ref[pl.ds(j, rows, stride=ROW_SUB), :] for j in range(ROW_SUB)], axis=1)


def _mod_spec(mod, tm, width, col_block):
    if mod.shape[1] == 1:
        return pl.BlockSpec((1, 1, width), lambda g, i: (g, 0, col_block))
    return pl.BlockSpec((1, tm, width), lambda g, i: (g, i, col_block))


def _layer_spec(a, layer):
    return pl.BlockSpec((1,) + a.shape[1:], lambda g, i: (layer,) + (0,) * (a.ndim - 1))


def _mod_kernel(c_ref, w_ref, b_ref, o_ref):
    c = c_ref[...]
    a = (c * _sigmoid(c)).astype(BF16)
    o_ref[0] = jnp.dot(a, w_ref[0].astype(BF16), preferred_element_type=F32) + b_ref[0]


def adaln_mod(c_all, w_mod, b_mod):
    depth, d, n = w_mod.shape
    m = c_all.shape[0]
    tn = 1536
    return pl.pallas_call(
        _mod_kernel,
        grid=(depth, n // tn),
        in_specs=[pl.BlockSpec((m, d), lambda l, j: (0, 0)),
                  pl.BlockSpec((1, d, tn), lambda l, j: (l, 0, j)),
                  pl.BlockSpec((1, 1, tn), lambda l, j: (l, 0, j))],
        out_specs=pl.BlockSpec((1, m, tn), lambda l, j: (l, 0, j)),
        out_shape=jax.ShapeDtypeStruct((depth, m, n), F32),
        compiler_params=_cparams("arbitrary", "arbitrary"),
        name="adaln_mod",
    )(c_all, w_mod, b_mod.reshape(depth, 1, n))


def _conv_ln_swish(sh_ref, prev, cur, w_ref, cb, g, b, o_ref):
    for step in _conv_steps(sh_ref, prev, cur, w_ref, cb, g, b, o_ref):
        step()


def _conv_steps(sh_ref, prev, cur, w_ref, cb, g, b, o_ref):
    t_rows = cur.shape[0]
    n = CONV_HALO + t_rows
    rows = min(CONV_ROWS, t_rows)
    off = CONV_HALO - (CONV_WIDTH - 1)
    ngrp = rows // VREG_SUBLANES

    def setup():
        sh_ref[0, 0:CONV_HALO, :] = prev
        sh_ref[0, CONV_HALO:n, :] = cur
        for s in range(1, VREG_SUBLANES):
            sh_ref[s, 0:n - VREG_SUBLANES, :] = sh_ref[0, pl.ds(s, n - VREG_SUBLANES), :]

    def block(r0):
        accs = [None] * ngrp
        for w in range(CONV_WIDTH):
            s = (w + off) % VREG_SUBLANES
            a = r0 + (w + off) - s
            w8 = w_ref[0, w * VREG_SUBLANES:(w + 1) * VREG_SUBLANES, :]
            for k in range(ngrp):
                lo = a + k * VREG_SUBLANES
                term = sh_ref[s, lo:lo + VREG_SUBLANES, :] * w8
                accs[k] = term if accs[k] is None else accs[k] + term
        acc = jnp.concatenate(accs, axis=0) if ngrp > 1 else accs[0]
        z = _layer_norm(acc + cb, g, b)
        o_ref[0, r0:r0 + rows, :] = (z * _sigmoid(z)).astype(BF16)

    return [setup] + [functools.partial(block, r0) for r0 in range(0, t_rows, rows)]


def _in_proj_kernel(*refs, dils, windows):
    if windows is None:
        (x_ref, mod_ref, w_ref, b_ref, q0, q1, q2, kv0, kv1, kv2, u_ref, ga_ref, gb_ref, zs_ref) = refs
    else:
        (x_ref, mod_ref, w_ref, b_ref, wkv_ref, bkv_ref, cw_ref, cb_ref, cg_ref, cbeta_ref,
         q0, q1, q2, kv0, kv1, kv2, act_ref, ut_ref, ga_ref, gb_ref, t0, t1, t2,
         zs_ref, sh_ref, hist_ref) = refs
    x = x_ref[0]
    tm = x.shape[0]
    sh = mod_ref[0, :, 0:D_MODEL]
    sc = mod_ref[0, :, D_MODEL:2 * D_MODEL]
    h = (x * (1.0 + sc) + sh).astype(BF16)

    if windows is not None:
        i = pl.program_id(1)
        n_tiles = pl.num_programs(1)
        for g, t_ref in enumerate((t0, t1, t2)):
            @pl.when(i >= n_tiles - max(windows[g] // tm, 1))
            def _(g=g, t_ref=t_ref):
                bias = jnp.concatenate([bkv_ref[0, g]] * (tm // SPAD), axis=1)
                kvt = lax.dot_general(wkv_ref[0, g], h, _NT, preferred_element_type=F32) + bias
                t_ref[0] = kvt if windows[g] >= tm else kvt[:, tm - windows[g]:]

    def proj(c0, c1):
        return jnp.dot(h, w_ref[0, :, c0:c1], preferred_element_type=F32) + b_ref[0, :, c0:c1]

    def put(out_ref, col0, z, dil):
        if dil == 1:
            out_ref[0, :, col0:col0 + LANE_W] = z.astype(out_ref.dtype)
            return
        nt = LANE_W // VREG_LANES
        for c in range(nt):
            zs_ref[c] = z[:, c * VREG_LANES:(c + 1) * VREG_LANES]
        blk = out_ref.shape[2] // dil
        for r in range(dil):
            for c in range(nt):
                piece = zs_ref[c, pl.ds(r, tm // dil, stride=dil), :]
                lo = r * blk + col0 + c * VREG_LANES
                out_ref[0, :, lo:lo + VREG_LANES] = piece.astype(out_ref.dtype)

    o1 = 3 * ATT_W
    o2 = o1 + 2 * CONV_CH
    glu_a = proj(o1, o1 + CONV_CH)
    glu_b = proj(o1 + CONV_CH, o1 + 2 * CONV_CH)
    u = glu_a * _sigmoid(glu_b)
    conv_steps = []
    if windows is None:
        u_ref[0] = u
    else:
        prev = jnp.where(pl.program_id(1) == 0, 0.0, hist_ref[...])
        conv_steps = _conv_steps(sh_ref, prev, u, cw_ref, cb_ref[0], cg_ref[0], cbeta_ref[0], act_ref)
        conv_steps.pop(0)()
        hist_ref[...] = u[tm - CONV_HALO:tm]
        ut_ref[0] = u[tm - CONV_HALO:tm]

    jobs = []
    for g, (q_ref, kv_ref) in enumerate(((q0, kv0), (q1, kv1), (q2, kv2))):
        c = g * LANE_W
        jobs.append(functools.partial(
            lambda q_ref, c, g: put(q_ref, 0, proj(c, c + LANE_W) * (HEAD_DIM ** -0.5), dils[g]), q_ref, c, g))
        jobs.append(functools.partial(
            lambda kv_ref, c, g: put(kv_ref, 0, proj(ATT_W + c, ATT_W + c + LANE_W), dils[g]), kv_ref, c, g))
        jobs.append(functools.partial(
            lambda kv_ref, c, g: put(kv_ref, LANE_W, proj(2 * ATT_W + c, 2 * ATT_W + c + LANE_W), dils[g]),
            kv_ref, c, g))
    half = D_MODEL // 2
    for gate_ref, c0 in ((ga_ref, o2), (gb_ref, o2 + D_MODEL)):
        for hc in (0, half):
            def gate_job(gate_ref=gate_ref, c0=c0, hc=hc):
                gate_ref[0, :, hc:hc + half] = _sigmoid(proj(c0 + hc, c0 + hc + half)).astype(BF16)
            jobs.append(gate_job)
    per_job = -(-len(conv_steps) // len(jobs))
    for job in jobs:
        job()
        for _ in range(min(per_job, len(conv_steps))):
            conv_steps.pop(0)()


def in_proj(x, mod, w_bf, b, layer, tm, dils, wkv_t=None, bkv_t=None, windows=None, conv=None):
    g_, r, d = x.shape
    n_tiles = r // tm
    row = lambda w: pl.BlockSpec((1, tm, w), lambda g, i: (g, i, 0))
    fold = lambda w, dil: pl.BlockSpec((1, tm // dil, dil * w), lambda g, i: (g, i, 0))
    sds = lambda w, dt: jax.ShapeDtypeStruct((g_, r, w), dt)
    fsds = lambda w, dil, dt: jax.ShapeDtypeStruct((g_, r // dil, dil * w), dt)
    const = lambda a: pl.BlockSpec((1,) + a.shape[1:], lambda g, i: (layer,) + (0,) * (a.ndim - 1),
                                   pipeline_mode=pl.Buffered(1))
    in_specs = [row(d), _mod_spec(mod, tm, 2 * d, 0), const(w_bf), _layer_spec(b, layer)]
    args = [x, mod, w_bf, b]
    qkv_specs = [fold(LANE_W, dl) for dl in dils] + [fold(2 * LANE_W, dl) for dl in dils]
    qkv_shape = [fsds(LANE_W, dl, BF16) for dl in dils] + [fsds(2 * LANE_W, dl, F32) for dl in dils]
    scratch = [pltpu.VMEM((LANE_W // VREG_LANES, tm, VREG_LANES), F32)]
    if windows is None:
        out_specs = qkv_specs + [row(CONV_CH), row(d), row(d)]
        out_shape = qkv_shape + [sds(CONV_CH, F32), sds(d, BF16), sds(d, BF16)]
    else:
        in_specs += [const(wkv_t), _layer_spec(bkv_t, layer)] + [_layer_spec(a, layer) for a in conv]
        args += [wkv_t, bkv_t, *conv]
        out_specs = qkv_specs + [row(CONV_CH), pl.BlockSpec((1, CONV_HALO, CONV_CH), lambda g, i: (g, 0, 0)),
                                 row(d), row(d)]
        out_shape = qkv_shape + [sds(CONV_CH, BF16), jax.ShapeDtypeStruct((g_, CONV_HALO, CONV_CH), F32),
                                 sds(d, BF16), sds(d, BF16)]
        scratch += [pltpu.VMEM((VREG_SUBLANES, CONV_HALO + tm, CONV_CH), F32),
                    pltpu.VMEM((CONV_HALO, CONV_CH), F32)]
        for w in windows:
            assert w % tm == 0 or tm % w == 0
            first = n_tiles - max(w // tm, 1)
            out_specs.append(pl.BlockSpec((1, 2 * LANE_W, min(w, tm)),
                                          lambda g, i, first=first: (g, 0, jnp.maximum(i - first, 0))))
            out_shape.append(jax.ShapeDtypeStruct((g_, 2 * LANE_W, w), F32))
    return pl.pallas_call(
        functools.partial(_in_proj_kernel, dils=dils, windows=windows),
        grid=(g_, n_tiles),
        in_specs=in_specs,
        out_specs=out_specs,
        out_shape=out_shape,
        scratch_shapes=scratch,
        compiler_params=_cparams("arbitrary", "arbitrary"),
        name="in_proj",
    )(*args)


def _attn_kernel(q_ref, kvp_ref, kvc_ref, bias_ref, o_ref, l_ref, *, qb):
    i = pl.program_id(2)
    head_of_lane = lax.broadcasted_iota(I32, (QBLK, LANE_W), 1) >> 6
    ones = jnp.ones((2 * QBLK, VREG_LANES), BF16)
    first_cols = lax.broadcasted_iota(I32, (LANES * QBLK, 2 * QBLK), 1) < QBLK
    for j in range(qb):
        rs = slice(j * QBLK, (j + 1) * QBLK)
        q = q_ref[0, rs, :]
        q4 = jnp.concatenate([jnp.where(head_of_lane == h, q, jnp.zeros_like(q)) for h in range(LANES)], axis=0)
        if j == 0:
            kv = jnp.concatenate([kvp_ref[0], kvc_ref[0, rs, :]], axis=0)
        else:
            kv = kvc_ref[0, (j - 1) * QBLK:(j + 1) * QBLK, :]
        k2, v2 = kv[:, 0:LANE_W].astype(BF16), kv[:, LANE_W:2 * LANE_W].astype(BF16)
        s = lax.dot_general(q4, k2, _NT, preferred_element_type=F32) + bias_ref[...]
        if j == 0:
            s = jnp.where(jnp.logical_and(first_cols, i == 0), NEG_INF, s)
        m = jnp.max(s, axis=-1, keepdims=True)
        p = jnp.exp(s - m).astype(BF16)
        l = jnp.dot(p, ones, preferred_element_type=F32)
        o4 = jnp.dot(p, v2, preferred_element_type=F32)
        inv = 1.0 / l
        o4 = o4 * jnp.concatenate([inv, inv], axis=1)
        lse = m + jnp.log(l)
        lse2 = jnp.concatenate([lse, lse], axis=1)
        o, ls = o4[0:QBLK], lse2[0:QBLK]
        for h in range(1, LANES):
            sel = head_of_lane == h
            o = jnp.where(sel, o4[h * QBLK:(h + 1) * QBLK], o)
            ls = jnp.where(sel, lse2[h * QBLK:(h + 1) * QBLK], ls)
        o_ref[0, rs, :] = o.astype(BF16)
        l_ref[0, rs, :] = ls


def _band_bias(slopes_g, dil):
    qi = jnp.arange(QBLK)[:, None]
    kj = jnp.arange(QBLK)[None, :]
    dist_a = qi + BAND - kj
    dist_b = qi - kj
    sl = -(slopes_g * dil)[:, None, None]
    ba = jnp.where((dist_a <= BAND)[None], sl * dist_a.astype(F32)[None], NEG_INF)
    bb = jnp.where((dist_b >= 0)[None], sl * dist_b.astype(F32)[None], NEG_INF)
    return jnp.concatenate([ba, bb], axis=2).astype(F32).reshape(LANES * QBLK, 2 * QBLK)


def attn_prompt(qf, kvf, slopes_g, dil):
    b_, l_, _ = qf.shape
    qb = 4 if l_ % (4 * QBLK) == 0 else (2 if l_ % (2 * QBLK) == 0 else 1)
    nb = l_ // (qb * QBLK)
    bias = _band_bias(slopes_g, dil)
    bias_spec = pl.BlockSpec((LANES * QBLK, 2 * QBLK), lambda b, r, i: (0, 0))
    return pl.pallas_call(
        functools.partial(_attn_kernel, qb=qb),
        grid=(b_, dil, nb),
        in_specs=[pl.BlockSpec((1, qb * QBLK, LANE_W), lambda b, r, i: (b, i, r)),
                  pl.BlockSpec((1, QBLK, 2 * LANE_W), lambda b, r, i: (b, jnp.maximum(i * qb - 1, 0), r)),
                  pl.BlockSpec((1, qb * QBLK, 2 * LANE_W), lambda b, r, i: (b, i, r)),
                  bias_spec],
        out_specs=[pl.BlockSpec((1, qb * QBLK, LANE_W), lambda b, r, i: (b, i, r))] * 2,
        out_shape=[jax.ShapeDtypeStruct((b_, l_, dil * LANE_W), BF16),
                   jax.ShapeDtypeStruct((b_, l_, dil * LANE_W), F32)],
        compiler_params=_cparams("arbitrary", "arbitrary", "arbitrary"),
        name="attn_prompt",
    )(qf, kvf, kvf, bias)


SQ = 8
SPAD = 128


def _attn_s_kernel(x_ref, mod_ref, wq_ref, bq_ref, wkv_ref, bkv_ref, cache_ref, bias_ref, _tails_in,
                   o_ref, l_ref, tail_ref, hpad_ref, *, lw, t_new):
    nq = LANES * SQ
    blk = 2 * SQ

    @pl.when(pl.program_id(0) == 0)
    def _():
        hpad_ref[0:SPAD - blk, :] = jnp.zeros((SPAD - blk, D_MODEL), BF16)

    sh = mod_ref[0, :, 0:D_MODEL]
    sc = mod_ref[0, :, D_MODEL:2 * D_MODEL]
    h8 = x_ref[0] * (1.0 + sc) + sh
    h16 = jnp.concatenate([jnp.zeros_like(h8), h8], axis=0).astype(BF16)
    hpad_ref[SPAD - blk:SPAD, :] = h16
    q16 = (jnp.dot(h16, wq_ref[0], preferred_element_type=F32) + bq_ref[0]) * (HEAD_DIM ** -0.5)
    q8 = q16[SQ:blk].astype(BF16)
    head_of_lane = lax.broadcasted_iota(I32, (SQ, LANE_W), 1) >> 6
    q4 = jnp.concatenate([jnp.where(head_of_lane == h, q8, jnp.zeros_like(q8)) for h in range(LANES)], axis=0)
    kvn = lax.dot_general(wkv_ref[0, 0], hpad_ref[...], _NT, preferred_element_type=F32) + bkv_ref[0, 0]
    kc = cache_ref[0, 0, 0:LANE_W, :].astype(BF16)
    vc = cache_ref[0, 0, LANE_W:2 * LANE_W, :].astype(BF16)
    s_c = jnp.dot(q4, kc, preferred_element_type=F32) + bias_ref[:, 0:lw]
    s_n = jnp.dot(q4, kvn[0:LANE_W].astype(BF16), preferred_element_type=F32) + bias_ref[:, lw:lw + SPAD]
    m = jnp.maximum(jnp.max(s_c, axis=-1, keepdims=True), jnp.max(s_n, axis=-1, keepdims=True))
    p_c = jnp.exp(s_c - m)
    p_n = jnp.exp(s_n - m)
    l = jnp.sum(p_c, axis=-1, keepdims=True) + jnp.sum(p_n, axis=-1, keepdims=True)
    o4 = (lax.dot_general(p_c.astype(BF16), vc, _NT, preferred_element_type=F32)
          + lax.dot_general(p_n.astype(BF16), kvn[LANE_W:2 * LANE_W].astype(BF16), _NT,
                            preferred_element_type=F32)) / l
    lf = jnp.broadcast_to(m + jnp.log(l), (nq, LANE_W))
    rows = lax.broadcasted_iota(I32, (nq, LANE_W), 0)
    lanes = lax.broadcasted_iota(I32, (nq, LANE_W), 1)
    own = (rows >> 3) == (lanes >> 6)
    o4 = jnp.where(own, o4, 0.0)
    lf = jnp.where(own, lf, 0.0)
    o_ref[0] = (o4[0:SQ] + o4[SQ:2 * SQ] + o4[2 * SQ:3 * SQ] + o4[3 * SQ:4 * SQ]).astype(BF16)
    l_ref[0] = lf[0:SQ] + lf[SQ:2 * SQ] + lf[2 * SQ:3 * SQ] + lf[3 * SQ:4 * SQ]
    rolled = pltpu.roll(cache_ref[0, 0], lw - t_new, 1)
    if lw > SPAD:
        tail_ref[0, 0, :, 0:lw - SPAD] = rolled[:, 0:lw - SPAD]
    lane = lax.broadcasted_iota(I32, (2 * LANE_W, SPAD), 1)
    tail_ref[0, 0, :, lw - SPAD:lw] = jnp.where(lane >= SPAD - t_new, kvn, rolled[:, lw - SPAD:lw])


def _sample_bias(slopes_g, dil, lw, t_new):
    t = jnp.maximum(jnp.arange(SQ) - (SQ - t_new), 0)[:, None]
    r = jnp.arange(lw + SPAD)[None, :]
    pos = jnp.where(r < lw, r, r - (SPAD - t_new))
    dist = lw + t - pos
    jn = dist // dil
    real = (r < lw) | (r >= lw + SPAD - t_new)
    valid = real & (dist >= 0) & (dist % dil == 0) & (jn <= BAND)
    sl = (slopes_g * dil)[:, None, None]
    bias = jnp.where(valid[None], -sl * jn.astype(F32)[None], NEG_INF)
    return bias.reshape(LANES * SQ, lw + SPAD).astype(F32)


def attn_sample(x8, mod, w_bf, b3, wkv_t, bkv_t, cache_t, tails, layer, g, slopes_g, dil, t_new):
    db = x8.shape[0]
    depth, _, ch, lw = cache_t.shape
    bias = _sample_bias(slopes_g, dil, lw, t_new)
    in_specs = [pl.BlockSpec((1, SQ, D_MODEL), lambda b: (b, 0, 0)),
                pl.BlockSpec((1, 1, 2 * D_MODEL), lambda b: (b, 0, 0)),
                pl.BlockSpec((1, D_MODEL, LANE_W), lambda b: (layer, 0, g)),
                pl.BlockSpec((1, 1, LANE_W), lambda b: (layer, 0, g)),
                pl.BlockSpec((1, 1, ch, D_MODEL), lambda b: (layer, g, 0, 0)),
                pl.BlockSpec((1, 1, ch, SPAD), lambda b: (layer, g, 0, 0)),
                pl.BlockSpec((1, 1, ch, lw), lambda b: (layer, b, 0, 0)),
                pl.BlockSpec((LANES * SQ, lw + SPAD), lambda b: (0, 0)),
                pl.BlockSpec(memory_space=pl.ANY)]
    args = [x8, mod, w_bf, b3, wkv_t, bkv_t, cache_t, bias, tails]
    aliases = {len(args) - 1: 2}
    return pl.pallas_call(
        functools.partial(_attn_s_kernel, lw=lw, t_new=t_new),
        grid=(db,),
        in_specs=in_specs,
        out_specs=[pl.BlockSpec((1, SQ, LANE_W), lambda b: (b, 0, 0)),
                   pl.BlockSpec((1, SQ, LANE_W), lambda b: (b, 0, 0)),
                   pl.BlockSpec((1, 1, ch, lw), lambda b: (layer, b, 0, 0))],
        out_shape=[jax.ShapeDtypeStruct((db, SQ, LANE_W), BF16),
                   jax.ShapeDtypeStruct((db, SQ, LANE_W), F32),
                   jax.ShapeDtypeStruct((depth, db, ch, lw), F32)],
        scratch_shapes=[pltpu.VMEM((SPAD, D_MODEL), BF16)],
        input_output_aliases=aliases,
        compiler_params=_cparams("arbitrary"),
        name="attn_sample",
    )(*args)


CONV_ROWS = 16


def _conv_kernel(prev_ref, cur_ref, w_ref, cb_ref, g_ref, b_ref, o_ref, sh_ref, *, zero_first):
    prev = prev_ref[0]
    if zero_first:
        prev = jnp.where(pl.program_id(1) == 0, 0.0, prev)
    _conv_ln_swish(sh_ref, prev, cur_ref[0], w_ref, cb_ref[0], g_ref[0], b_ref[0], o_ref)


def conv_branch(u, prev_src, conv_w, conv_b, ln_g, ln_b, layer, tm, zero_first):
    g_, r, c = u.shape
    per = tm // CONV_HALO
    if zero_first:
        prev_map = lambda g, i: (g, jnp.maximum(i * per - 1, 0), 0)
    else:
        prev_map = lambda g, i: (g, 0, 0)
    return pl.pallas_call(
        functools.partial(_conv_kernel, zero_first=zero_first),
        grid=(g_, r // tm),
        in_specs=[pl.BlockSpec((1, CONV_HALO, c), prev_map),
                  pl.BlockSpec((1, tm, c), lambda g, i: (g, i, 0)),
                  _layer_spec(conv_w, layer), _layer_spec(conv_b, layer),
                  _layer_spec(ln_g, layer), _layer_spec(ln_b, layer)],
        out_specs=pl.BlockSpec((1, tm, c), lambda g, i: (g, i, 0)),
        out_shape=jax.ShapeDtypeStruct((g_, r, c), BF16),
        scratch_shapes=[pltpu.VMEM((VREG_SUBLANES, CONV_HALO + tm, c), F32)],
        compiler_params=_cparams("arbitrary", "arbitrary"),
        name="conv_branch",
    )(prev_src, u, conv_w, conv_b, ln_g, ln_b)


def _mix_kernel(o0, o1, o2, l0, l1, l2, cb_ref, ga_ref, gb_ref, x_ref, mod_ref,
                woa_ref, boa_ref, wpb_ref, bpb_ref, wout_ref, bout_ref, g1_ref, b1_ref,
                wrh_ref, wrl_ref, br_ref, tri_ref, cin_ref, x1_ref, hu_ref, route_ref, cnt_ref,
                s0, s1, s2, s3, carry_ref, *, dils):
    tm = x_ref.shape[1]

    @pl.when(jnp.logical_and(pl.program_id(0) == 0, pl.program_id(1) == 0))
    def _():
        carry_ref[...] = cin_ref[...]

    def unfold(ref, dil, scratch):
        if dil == 1:
            return ref[0].astype(F32)
        nt = LANE_W // VREG_LANES
        for r in range(dil):
            for c in range(nt):
                lo = r * LANE_W + c * VREG_LANES
                scratch[c, pl.ds(r, tm // dil, stride=dil), :] = ref[0, :, lo:lo + VREG_LANES].astype(F32)
        return jnp.concatenate([scratch[c] for c in range(nt)], axis=1)

    assert dils[0] == 1
    oa, ob, oc = unfold(o0, 1, None), unfold(o1, dils[1], s0), unfold(o2, dils[2], s1)
    la, lb, lc = unfold(l0, 1, None), unfold(l1, dils[1], s2), unfold(l2, dils[2], s3)
    lmax = jnp.maximum(jnp.maximum(la, lb), lc)
    ea, eb, ec = jnp.exp(la - lmax), jnp.exp(lb - lmax), jnp.exp(lc - lmax)
    merged = (ea * oa + eb * ob + ec * oc) / (ea + eb + ec)
    br_a = jnp.dot(merged.astype(BF16), woa_ref[0], preferred_element_type=F32) + boa_ref[0]
    br_b = jnp.dot(cb_ref[0], wpb_ref[0], preferred_element_type=F32) + bpb_ref[0]
    mixed = ga_ref[0].astype(F32) * br_a + gb_ref[0].astype(F32) * br_b
    y = jnp.dot(mixed.astype(BF16), wout_ref[0], preferred_element_type=F32) + bout_ref[0]
    gt1 = mod_ref[0, :, 2 * D_MODEL:3 * D_MODEL]
    sh2 = mod_ref[0, :, 3 * D_MODEL:4 * D_MODEL]
    sc2 = mod_ref[0, :, 4 * D_MODEL:5 * D_MODEL]
    x1 = _layer_norm(ALPHA_DN * x_ref[0] + (1.0 + gt1) * y, g1_ref[0], b1_ref[0])
    x1_ref[0] = x1
    h2 = x1 * (1.0 + sc2) + sh2
    hi = h2.astype(BF16)
    lo = (h2 - hi.astype(F32)).astype(BF16)
    logits = (lax.dot_general(wrh_ref[0], hi, _NT, preferred_element_type=F32)
              + lax.dot_general(wrh_ref[0], lo, _NT, preferred_element_type=F32)
              + lax.dot_general(wrl_ref[0], hi, _NT, preferred_element_type=F32)
              + jnp.concatenate([br_ref[0]] * (tm // VREG_LANES), axis=1))
    _store_row_words(hu_ref.at[0], _pack_bf16_pair(h2[:, 0:PACK_W], h2[:, PACK_W:D_MODEL]))
    _route_tile(logits, tri_ref, carry_ref, route_ref, cnt_ref)


def mix(o_g, l_g, cb, ga, gb, x, mod, wts, counts_in, layer, tm, dils):
    g_, r, d = x.shape
    row = lambda w: pl.BlockSpec((1, tm, w), lambda g, i: (g, i, 0))
    fold = lambda dil: pl.BlockSpec((1, tm // dil, dil * LANE_W), lambda g, i: (g, i, 0))
    whole = lambda a: pl.BlockSpec(a.shape, lambda g, i: (0,) * a.ndim)
    tri = (jnp.arange(tm)[:, None] < jnp.arange(tm)[None, :]).astype(BF16)
    return pl.pallas_call(
        functools.partial(_mix_kernel, dils=dils),
        grid=(g_, r // tm),
        in_specs=[fold(dl) for dl in dils] * 2 + [row(CONV_CH), row(d), row(d), row(d), _mod_spec(mod, tm, 6 * d, 0)]
                 + [_layer_spec(a, layer) for a in wts] + [whole(tri), whole(counts_in)],
        out_specs=[row(d), pl.BlockSpec((1, tm * ROW_SUB, ROW_LANE), lambda g, i: (g, i, 0)),
                   pl.BlockSpec((1, ROUTE_ROWS, tm), lambda g, i: (g, 0, i)),
                   pl.BlockSpec((N_EXPERTS, ROUTER_LANES), lambda g, i: (0, 0))],
        out_shape=[jax.ShapeDtypeStruct((g_, r, d), F32),
                   jax.ShapeDtypeStruct((g_, r * ROW_SUB, ROW_LANE), U32),
                   jax.ShapeDtypeStruct((g_, ROUTE_ROWS, r), F32),
                   jax.ShapeDtypeStruct((N_EXPERTS, ROUTER_LANES), F32)],
        scratch_shapes=[pltpu.VMEM((LANE_W // VREG_LANES, tm, VREG_LANES), F32)] * 4
                       + [pltpu.VMEM((N_EXPERTS, ROUTER_LANES), F32)],
        compiler_params=_cparams("arbitrary", "arbitrary"),
        name="mix",
    )(*o_g, *l_g, cb, ga, gb, x, mod, *wts, tri, counts_in)


ROUTE_ROWS = 16


def _route_tile(work, tri_ref, carry_ref, out_ref, cnt_ref):
    tm = work.shape[1]
    row = lax.broadcasted_iota(I32, (N_EXPERTS, tm), 0)
    vals, idxs, hots = [], [], []
    for _ in range(TOP_K):
        m = jnp.max(work, axis=0, keepdims=True)
        idx = jnp.min(jnp.where(work == m, row, N_EXPERTS), axis=0, keepdims=True)
        hot = row == idx
        work = jnp.where(hot, -jnp.inf, work)
        vals.append(m)
        idxs.append(idx)
        hots.append(hot)
    es = [jnp.exp(v - vals[0]) for v in vals]
    den = es[0] + es[1] + es[2] + es[3]
    member = jnp.where(hots[0] | hots[1] | hots[2] | hots[3], 1.0, 0.0)
    carry = carry_ref[...]
    before = (jnp.dot(member.astype(BF16), tri_ref[...], preferred_element_type=F32)
              + jnp.concatenate([carry] * (tm // VREG_LANES), axis=1))
    rec_row = lax.broadcasted_iota(I32, (ROUTE_ROWS, tm), 0)
    out = jnp.zeros((ROUTE_ROWS, tm), F32)
    for k in range(TOP_K):
        rank = jnp.sum(jnp.where(hots[k], before, 0.0), axis=0, keepdims=True)
        out = jnp.where(rec_row == k, es[k] / den, out)
        out = jnp.where(rec_row == TOP_K + k, idxs[k].astype(F32), out)
        out = jnp.where(rec_row == 2 * TOP_K + k, rank, out)
    out_ref[0] = out
    carry_ref[...] = carry + jnp.sum(member, axis=1, keepdims=True)
    cnt_ref[...] = carry_ref[...]


def _moe_kernel(be_ref, nu_ref, half_ref, xs_ref, wup_ref, bup_ref, wdn_ref, bdn_ref, y_ref,
                wup_bf, wdn_bf, act_ref):
    i = pl.program_id(0)

    def expert_mlp(rows):
        words = jnp.concatenate([xs_ref[pl.ds(j, rows, stride=ROW_SUB), :] for j in range(ROW_SUB)], axis=1)
        lo, hi = _unpack_bf16_pair(words)
        x = jnp.concatenate([lo, hi], axis=1).astype(BF16)
        cw = 512
        for c0 in range(0, D_FF, cw):
            a = jnp.dot(x, wup_bf[:, c0:c0 + cw], preferred_element_type=F32) + bup_ref[0, 0, :, c0:c0 + cw]
            lin = (jnp.dot(x, wup_bf[:, D_FF + c0:D_FF + c0 + cw], preferred_element_type=F32)
                   + bup_ref[0, 0, :, D_FF + c0:D_FF + c0 + cw])
            a = jnp.minimum(a, SWIGLU_LIMIT)
            lin = jnp.clip(lin, -SWIGLU_LIMIT, SWIGLU_LIMIT)
            act_ref[0:rows, c0:c0 + cw] = (a * _sigmoid(SWIGLU_ALPHA * a) * (lin + 1.0)).astype(BF16)
        y = jnp.dot(act_ref[0:rows, :], wdn_bf[...], preferred_element_type=F32) + bdn_ref[0, 0]
        words = _pack_bf16_pair_native(y[:, 0:PACK_W], y[:, PACK_W:D_MODEL])
        for j in range(ROW_SUB):
            y_ref[pl.ds(j, rows, stride=ROW_SUB), :] = words[:, j * ROW_LANE:(j + 1) * ROW_LANE]

    @pl.when(i < nu_ref[0])
    def _():
        prev_e = be_ref[jnp.maximum(i - 1, 0)]

        @pl.when((i == 0) | (be_ref[i] != prev_e))
        def _():
            step = 128
            for r0 in range(0, D_MODEL, step):
                wup_bf[r0:r0 + step, :] = wup_ref[0, 0, r0:r0 + step, :].astype(BF16)
            for r0 in range(0, D_FF, step):
                wdn_bf[r0:r0 + step, :] = wdn_ref[0, 0, r0:r0 + step, :].astype(BF16)

        @pl.when(half_ref[i] == 0)
        def _():
            expert_mlp(MOE_TM)

        @pl.when(half_ref[i] != 0)
        def _():
            expert_mlp(MOE_TM // 2)


def moe_experts(xs, block_e, n_used, half, w_up, b_up, w_down, b_down, layer):
    cap = xs.shape[0] // ROW_SUB
    nblk = cap // MOE_TM
    row_map = lambda i, be, nu, hf: (jnp.minimum(i, nu[0] - 1), 0)
    exp_map = lambda i, be, nu, hf: (layer, be[i], 0, 0)
    grid_spec = pltpu.PrefetchScalarGridSpec(
        num_scalar_prefetch=3,
        grid=(nblk,),
        in_specs=[pl.BlockSpec((MOE_TM * ROW_SUB, ROW_LANE), row_map),
                  pl.BlockSpec((1, 1, D_MODEL, 2 * D_FF), exp_map),
                  pl.BlockSpec((1, 1, 1, 2 * D_FF), exp_map),
                  pl.BlockSpec((1, 1, D_FF, D_MODEL), exp_map),
                  pl.BlockSpec((1, 1, 1, D_MODEL), exp_map)],
        out_specs=pl.BlockSpec((MOE_TM * ROW_SUB, ROW_LANE), row_map),
        scratch_shapes=[pltpu.VMEM((D_MODEL, 2 * D_FF), BF16),
                        pltpu.VMEM((D_FF, D_MODEL), BF16),
                        pltpu.VMEM((MOE_TM, D_FF), BF16)],
    )
    return pl.pallas_call(
        _moe_kernel,
        grid_spec=grid_spec,
        out_shape=jax.ShapeDtypeStruct((cap * ROW_SUB, ROW_LANE), U32),
        compiler_params=_cparams("arbitrary"),
        name="moe_experts",
    )(block_e, n_used, half, xs, w_up, b_up, w_down, b_down)


def _final_kernel(y0, y1, y2, y3, r_ref, x_ref, mod_ref, g2_ref, b2_ref, o_ref):
    gates = r_ref[...]
    acc_lo = None
    for k, y_ref in enumerate((y0, y1, y2, y3)):
        lo, hi = _unpack_bf16_pair(_load_row_words(y_ref))
        gk = gates[:, k:k + 1]
        acc_lo = gk * lo if acc_lo is None else acc_lo + gk * lo
        acc_hi = gk * hi if k == 0 else acc_hi + gk * hi
    y = jnp.concatenate([acc_lo, acc_hi], axis=1)
    gt2 = mod_ref[0]
    o_ref[0] = _layer_norm(ALPHA_DN * x_ref[0] + (1.0 + gt2) * y, g2_ref[0], b2_ref[0])


def final(yg, route, x1, mod, ln_g, ln_b, layer, tm, row0):
    g_, r, d = x1.shape
    per = r // tm
    base = row0 // tm
    nblk_k = yg.shape[0] // (TOP_K * tm * ROW_SUB)
    yspec = lambda k: pl.BlockSpec((tm * ROW_SUB, ROW_LANE), lambda g, i: (k * nblk_k + base + g * per + i, 0))
    return pl.pallas_call(
        _final_kernel,
        grid=(g_, per),
        in_specs=[yspec(0), yspec(1), yspec(2), yspec(3),
                  pl.BlockSpec((tm, ROUTER_LANES), lambda g, i: (g * per + i, 0)),
                  pl.BlockSpec((1, tm, d), lambda g, i: (g, i, 0)), _mod_spec(mod, tm, d, 5),
                  _layer_spec(ln_g, layer), _layer_spec(ln_b, layer)],
        out_specs=pl.BlockSpec((1, tm, d), lambda g, i: (g, i, 0)),
        out_shape=jax.ShapeDtypeStruct((g_, r, d), F32),
        compiler_params=_cparams("arbitrary", "arbitrary"),
        name="final",
    )(yg, yg, yg, yg, route, x1, mod, ln_g, ln_b)


SC_CORES = 2
SC_SUBCORES = 16
SC_WORKERS = SC_CORES * SC_SUBCORES
SC_CHUNK = 128


def _sc_mesh():
    return plsc.VectorSubcoreMesh(core_axis_name="c", subcore_axis_name="s")


def _sc_worker_chunks(nchunk):
    wid = lax.axis_index("s") * SC_CORES + lax.axis_index("c")
    return wid, (nchunk - wid + SC_WORKERS - 1) // SC_WORKERS


def dispatch_rows(hu_a, hu_b, dest4, cap):
    chunks_a = hu_a.shape[0] // SC_CHUNK
    nchunk = chunks_a + hu_b.shape[0] // SC_CHUNK
    idx = dest4.reshape(TOP_K, nchunk, SC_CHUNK)

    @functools.partial(
        pl.kernel, mesh=_sc_mesh(),
        out_type=jax.ShapeDtypeStruct((cap, ROW_SUB, ROW_LANE), U32),
        scratch_types=[pltpu.VMEM((SC_CHUNK,), I32)] * TOP_K
                      + [pltpu.VMEM((SC_CHUNK, ROW_SUB, ROW_LANE), U32)])
    def k(a_hbm, b_hbm, i_hbm, o_hbm, i0, i1, i2, i3, rows_v):
        wid, n_mine = _sc_worker_chunks(nchunk)

        @pl.loop(0, n_mine)
        def _(j):
            c = wid + j * SC_WORKERS

            @pl.when(c < chunks_a)
            def _():
                pltpu.sync_copy(a_hbm.at[pl.ds(c * SC_CHUNK, SC_CHUNK)], rows_v)

            @pl.when(c >= chunks_a)
            def _():
                pltpu.sync_copy(b_hbm.at[pl.ds((c - chunks_a) * SC_CHUNK, SC_CHUNK)], rows_v)

            for kk, iv in enumerate((i0, i1, i2, i3)):
                pltpu.sync_copy(i_hbm.at[kk, c], iv)
            for iv in (i0, i1, i2, i3):
                pltpu.sync_copy(rows_v, o_hbm.at[iv])

    return k(hu_a, hu_b, idx)


def gather_rows(y, dest4):
    m = dest4.shape[0] * dest4.shape[1]
    nchunk = m // SC_CHUNK
    idx = dest4.reshape(nchunk, SC_CHUNK)

    @functools.partial(
        pl.kernel, mesh=_sc_mesh(),
        out_type=jax.ShapeDtypeStruct((m, ROW_SUB, ROW_LANE), U32),
        scratch_types=[pltpu.VMEM((SC_CHUNK,), I32), pltpu.VMEM((SC_CHUNK, ROW_SUB, ROW_LANE), U32)])
    def k(y_hbm, i_hbm, o_hbm, idx_v, rows_v):
        wid, n_mine = _sc_worker_chunks(nchunk)

        @pl.loop(0, n_mine)
        def _(j):
            c = wid + j * SC_WORKERS
            pltpu.sync_copy(i_hbm.at[c], idx_v)
            pltpu.sync_copy(y_hbm.at[idx_v], rows_v)
            pltpu.sync_copy(rows_v, o_hbm.at[pl.ds(c * SC_CHUNK, SC_CHUNK)])

    return k(y, idx)


def routing_plan(rec, counts, n_tok):
    top_i = rec[TOP_K:2 * TOP_K].astype(I32)
    rank = rec[2 * TOP_K:3 * TOP_K].astype(I32)
    cnt = counts[:, 0].astype(I32)
    padded = ((cnt + MOE_TM - 1) // MOE_TM) * MOE_TM
    pend = jnp.cumsum(padded)
    pstart = pend - padded
    hot = top_i[:, :, None] == jnp.arange(N_EXPERTS, dtype=I32)[None, None, :]
    dest = rank + jnp.sum(jnp.where(hot, pstart[None, None, :], 0), axis=-1)
    nblk = -(-(n_tok * TOP_K) // MOE_TM) + N_EXPERTS
    n_used = pend[-1] // MOE_TM
    blk = jnp.arange(nblk, dtype=I32)
    blk_c = jnp.minimum(blk, n_used - 1)
    block_e = jnp.sum((blk_c[:, None] * MOE_TM >= pend[None, :]).astype(I32), axis=1)
    block_e = jnp.minimum(block_e, N_EXPERTS - 1)
    hot_e = block_e[:, None] == jnp.arange(N_EXPERTS, dtype=I32)[None, :]
    valid = jnp.sum(jnp.where(hot_e, (pstart + cnt)[None, :], 0), axis=1) - blk_c * MOE_TM
    half = (valid <= MOE_TM // 2).astype(I32)
    return dest, block_e, n_used.reshape(1).astype(I32), half, nblk * MOE_TM


def _alibi_slopes():
    return jnp.exp2(-8.0 * (jnp.arange(N_HEADS, dtype=F32) + 1.0) / N_HEADS)


def _row3(a):
    return a.reshape(a.shape[0], 1, a.shape[1])


def kernel(x_prompt, x_sample, cache_win0, cache_win1, cache_win2, state_conv, c_prompt, c_sample,
           w_mod, b_mod, w_in, b_in, w_oa, b_oa, conv_w, conv_b, conv_ln_g, conv_ln_b, w_pb, b_pb,
           w_out, b_out, ln1_g, ln1_b, w_router, b_router, w_up, b_up, w_down, b_down, ln2_g, ln2_b):
    depth = w_in.shape[0]
    b_, s, d = x_prompt.shape
    db, t_new, _ = x_sample.shape
    n_p, n_s = b_ * s, db * t_new
    n_tok = n_p + n_s
    tm_p = 512
    no_fold = (1,) * N_GROUPS
    slopes = _alibi_slopes()
    caches = [jnp.transpose(c, (0, 1, 3, 4, 5, 2)).reshape(c.shape[0], c.shape[1], 2 * LANE_W, c.shape[2])
              for c in (cache_win0, cache_win1, cache_win2)]
    wkv = w_in[:, :, ATT_W:3 * ATT_W].reshape(depth, D_MODEL, 2, N_GROUPS, LANE_W)
    wkv_t = jnp.transpose(wkv, (0, 3, 2, 4, 1)).reshape(depth, N_GROUPS, 2 * LANE_W, D_MODEL).astype(BF16)
    bkv = jnp.transpose(b_in[:, ATT_W:3 * ATT_W].reshape(depth, 2, N_GROUPS, LANE_W), (0, 2, 1, 3))
    bkv_t = jnp.broadcast_to(bkv.reshape(depth, N_GROUPS, 2 * LANE_W, 1), (depth, N_GROUPS, 2 * LANE_W, SPAD))

    mod_all = adaln_mod(jnp.concatenate([c_prompt, c_sample], axis=0), w_mod, b_mod)
    w_in_bf = w_in.astype(BF16)
    b_in3 = _row3(b_in)
    wr_t = jnp.transpose(w_router, (0, 2, 1))
    wr_hi = wr_t.astype(BF16)
    wr_lo = (wr_t - wr_hi.astype(F32)).astype(BF16)
    br_t = jnp.broadcast_to(b_router[:, :, None], (depth, N_EXPERTS, ROUTER_LANES))
    mix_wts = (w_oa.astype(BF16), _row3(b_oa), w_pb.astype(BF16), _row3(b_pb), w_out.astype(BF16), _row3(b_out),
               _row3(ln1_g), _row3(ln1_b), wr_hi, wr_lo, br_t)
    conv_b3, cg3, cb3 = _row3(conv_b), _row3(conv_ln_g), _row3(conv_ln_b)
    conv_w = jnp.repeat(conv_w, VREG_SUBLANES, axis=1)
    ln2g3, ln2b3 = _row3(ln2_g), _row3(ln2_b)
    e = w_up.shape[1]
    b_up4 = b_up.reshape(depth, e, 1, 2 * D_FF)
    b_down4 = b_down.reshape(depth, e, 1, D_MODEL)

    xp = x_prompt
    xs_ = x_sample.reshape(1, n_s, d)
    wp = [[] for _ in range(N_GROUPS)]
    ws = [jnp.zeros(c.shape, F32) for c in caches]
    cp, cs = [], []
    for l in range(depth):
        mod_p = mod_all[l, :b_].reshape(b_, 1, 6 * d)
        mod_s = jnp.repeat(mod_all[l, b_:], t_new, axis=0).reshape(1, n_s, 6 * d)

        (pq0, pq1, pq2, pkv0, pkv1, pkv2, pcb, put, pga, pgb, pt0, pt1, pt2) = in_proj(
            xp, mod_p, w_in_bf, b_in3, l, tm_p, DILATIONS, wkv_t, bkv_t, WINDOWS,
            (conv_w, conv_b3, cg3, cb3))
        pkv = (pkv0, pkv1, pkv2)
        po, pl_ = [], []
        for g, qg in enumerate((pq0, pq1, pq2)):
            o, lse = attn_prompt(qg, pkv[g], slopes[g * LANES:(g + 1) * LANES], DILATIONS[g])
            po.append(o)
            pl_.append(lse)
            wp[g].append((pt0, pt1, pt2)[g])
        cp.append(put[:, CONV_HALO - (CONV_WIDTH - 1):])

        sq0, sq1, sq2, skv0, skv1, skv2, su, sga, sgb = in_proj(xs_, mod_s, w_in_bf, b_in3, l, n_s, no_fold)
        del sq0, sq1, sq2, skv0, skv1, skv2
        x8 = jnp.pad(xs_.reshape(db, t_new, d), ((0, 0), (SQ - t_new, 0), (0, 0)))
        mod_seq = mod_all[l, b_:].reshape(db, 1, 6 * d)
        so, sl_ = [], []
        for g in range(N_GROUPS):
            o, lse, ws[g] = attn_sample(x8, mod_seq, w_in_bf, b_in3, wkv_t, bkv_t, caches[g], ws[g], l, g,
                                        slopes[g * LANES:(g + 1) * LANES], DILATIONS[g], t_new)
            so.append(o[:, SQ - t_new:].reshape(1, n_s, LANE_W))
            sl_.append(lse[:, SQ - t_new:].reshape(1, n_s, LANE_W))
        su3 = su.reshape(db, t_new, CONV_CH)
        full_s = jnp.concatenate([state_conv[l], su3], axis=1)
        cs.append(full_s[:, t_new:])
        st_pad = jnp.pad(state_conv[l], ((0, 0), (CONV_HALO - (CONV_WIDTH - 1), 0), (0, 0)))
        su8 = jnp.pad(su3, ((0, 0), (0, SQ - t_new), (0, 0)))
        scb = conv_branch(su8, st_pad, conv_w, conv_b3, cg3, cb3, l, SQ, False)
        scb = scb[:, :t_new].reshape(1, n_s, CONV_CH)

        px1, phu, rec_p, counts_p = mix(po, pl_, pcb, pga, pgb, xp, mod_p, mix_wts,
                                        jnp.zeros((N_EXPERTS, ROUTER_LANES), F32), l, tm_p, DILATIONS)
        sx1, shu, rec_s, counts = mix(so, sl_, scb, sga, sgb, xs_, mod_s, mix_wts, counts_p, l, n_s, no_fold)
        rec = jnp.concatenate([jnp.transpose(rec_p, (1, 0, 2)).reshape(ROUTE_ROWS, n_p), rec_s[0]], axis=1)
        gate_rows = jnp.pad(rec[0:TOP_K].T, ((0, 0), (0, ROUTER_LANES - TOP_K)))
        route_p, route_s = gate_rows[:n_p], gate_rows[n_p:]

        dest4, block_e, n_used, half, cap = routing_plan(rec, counts, n_tok)
        xsort = dispatch_rows(phu.reshape(n_p, ROW_SUB, ROW_LANE), shu.reshape(n_s, ROW_SUB, ROW_LANE), dest4, cap)
        ysort = moe_experts(xsort.reshape(cap * ROW_SUB, ROW_LANE), block_e, n_used, half,
                            w_up, b_up4, w_down, b_down4, l)
        n_pad = -(-n_tok // tm_p) * tm_p
        filler = (jnp.arange(TOP_K * (n_pad - n_tok), dtype=I32) * 997) % cap
        dest_pad = jnp.concatenate([dest4, filler.reshape(TOP_K, n_pad - n_tok)], axis=1)
        yg = gather_rows(ysort.reshape(cap, ROW_SUB, ROW_LANE), dest_pad)
        yg = yg.reshape(TOP_K * n_pad * ROW_SUB, ROW_LANE)

        xp = final(yg, route_p, px1, mod_p, ln2g3, ln2b3, l, tm_p, 0)
        xs_ = final(yg, route_s, sx1, mod_s, ln2g3, ln2b3, l, n_s, n_p)

    stack = lambda xs: jnp.stack(xs)
    to_win = lambda w: jnp.transpose(w.reshape(depth, w.shape[1], 2, LANES, HEAD_DIM, w.shape[-1]), (0, 1, 5, 2, 3, 4))
    return (xp, xs_.reshape(db, t_new, d),
            to_win(stack(wp[0])), to_win(stack(wp[1])), to_win(stack(wp[2])), stack(cp),
            to_win(ws[0]), to_win(ws[1]), to_win(ws[2]), stack(cs))
```

```python
import functools

import jax
import jax.numpy as jnp
from jax import lax
from jax.experimental import pallas as pl
from jax.experimental.pallas import tpu as pltpu
from jax.experimental.pallas import tpu_sc as plsc

F32 = jnp.float32
BF16 = jnp.bfloat16
U32 = jnp.uint32
I32 = jnp.int32

D_MODEL = 1024
N_GROUPS = 3
LANES = 4
HEAD_DIM = 64
LANE_W = LANES * HEAD_DIM
ATT_W = N_GROUPS * LANE_W
N_HEADS = N_GROUPS * LANES
WINDOWS = (128, 512, 2048)
DILATIONS = (1, 4, 16)
BAND = 128
CONV_CH = 512
CONV_WIDTH = 31
CONV_HALO = 32
N_EXPERTS = 32
TOP_K = 4
D_FF = 1024
SWIGLU_LIMIT = 7.0
SWIGLU_ALPHA = 1.702
N_IN = 3 * ATT_W + 2 * CONV_CH + 2 * D_MODEL
DEPTH_FOR_ALPHA = 4
ALPHA_DN = (2 * DEPTH_FOR_ALPHA) ** 0.25
LN_EPS = 1e-5
NEG_INF = -1e30

VREG_LANES = 128
VREG_SUBLANES = 8
QBLK = 128
ROUTER_LANES = VREG_LANES
MOE_TM = 512
PACK_W = D_MODEL // 2
ROW_LANE = VREG_LANES
ROW_SUB = PACK_W // ROW_LANE
VMEM_LIMIT = 56 * 1024 * 1024

_NT = (((1,), (1,)), ((), ()))


def _cparams(*sem):
    return pltpu.CompilerParams(dimension_semantics=sem, vmem_limit_bytes=VMEM_LIMIT)


def _sigmoid(x):
    return 1.0 / (1.0 + jnp.exp(-x))


def _layer_norm(r, g, b):
    mu = jnp.mean(r, axis=-1, keepdims=True)
    d = r - mu
    var = jnp.mean(d * d, axis=-1, keepdims=True)
    return d * lax.rsqrt(var + LN_EPS) * g + b


def _pack_bf16_pair(lo, hi):
    lo_b = lax.bitcast_convert_type(lo.astype(BF16).astype(F32), U32)
    hi_b = lax.bitcast_convert_type(hi.astype(BF16).astype(F32), U32)
    return (lo_b >> 16) | (hi_b & jnp.uint32(0xFFFF0000))


def _pack_bf16_pair_native(lo, hi):
    return pltpu.pack_elementwise([lo, hi], packed_dtype=BF16)


def _unpack_bf16_pair(w):
    lo = lax.bitcast_convert_type(w << 16, F32)
    hi = lax.bitcast_convert_type(w & jnp.uint32(0xFFFF0000), F32)
    return lo, hi


def _store_row_words(ref, words):
    rows = words.shape[0]
    for j in range(ROW_SUB):
        ref[pl.ds(j, rows, stride=ROW_SUB), :] = words[:, j * ROW_LANE:(j + 1) * ROW_LANE]


def _load_row_words(ref):
    rows = ref.shape[0] // ROW_SUB
    return jnp.concatenate([ref[pl.ds(j, rows, stride=ROW_SUB), :] for j in range(ROW_SUB)], axis=1)


def _mod_spec(mod, tm, width, col_block):
    if mod.shape[1] == 1:
        return pl.BlockSpec((1, 1, width), lambda g, i: (g, 0, col_block))
    return pl.BlockSpec((1, tm, width), lambda g, i: (g, i, col_block))


def _layer_spec(a, layer):
    return pl.BlockSpec((1,) + a.shape[1:], lambda g, i: (layer,) + (0,) * (a.ndim - 1))


def _mod_kernel(c_ref, w_ref, b_ref, o_ref):
    c = c_ref[...]
    a = (c * _sigmoid(c)).astype(BF16)
    o_ref[0] = jnp.dot(a, w_ref[0].astype(BF16), preferred_element_type=F32) + b_ref[0]


def adaln_mod(c_all, w_mod, b_mod):
    depth, d, n = w_mod.shape
    m = c_all.shape[0]
    tn = 1536
    return pl.pallas_call(
        _mod_kernel,
        grid=(depth, n // tn),
        in_specs=[pl.BlockSpec((m, d), lambda l, j: (0, 0)),
                  pl.BlockSpec((1, d, tn), lambda l, j: (l, 0, j)),
                  pl.BlockSpec((1, 1, tn), lambda l, j: (l, 0, j))],
        out_specs=pl.BlockSpec((1, m, tn), lambda l, j: (l, 0, j)),
        out_shape=jax.ShapeDtypeStruct((depth, m, n), F32),
        compiler_params=_cparams("arbitrary", "arbitrary"),
        name="adaln_mod",
    )(c_all, w_mod, b_mod.reshape(depth, 1, n))


def _conv_ln_swish(sh_ref, prev, cur, w_ref, cb, g, b, o_ref):
    for step in _conv_steps(sh_ref, prev, cur, w_ref, cb, g, b, o_ref):
        step()


def _conv_steps(sh_ref, prev, cur, w_ref, cb, g, b, o_ref):
    t_rows = cur.shape[0]
    n = CONV_HALO + t_rows
    rows = min(CONV_ROWS, t_rows)
    off = CONV_HALO - (CONV_WIDTH - 1)
    ngrp = rows // VREG_SUBLANES

    def setup():
        sh_ref[0, 0:CONV_HALO, :] = prev
        sh_ref[0, CONV_HALO:n, :] = cur
        for s in range(1, VREG_SUBLANES):
            sh_ref[s, 0:n - VREG_SUBLANES, :] = sh_ref[0, pl.ds(s, n - VREG_SUBLANES), :]

    def block(r0):
        accs = [None] * ngrp
        for w in range(CONV_WIDTH):
            s = (w + off) % VREG_SUBLANES
            a = r0 + (w + off) - s
            w8 = w_ref[0, w * VREG_SUBLANES:(w + 1) * VREG_SUBLANES, :]
            for k in range(ngrp):
                lo = a + k * VREG_SUBLANES
                term = sh_ref[s, lo:lo + VREG_SUBLANES, :] * w8
                accs[k] = term if accs[k] is None else accs[k] + term
        acc = jnp.concatenate(accs, axis=0) if ngrp > 1 else accs[0]
        z = _layer_norm(acc + cb, g, b)
        o_ref[0, r0:r0 + rows, :] = (z * _sigmoid(z)).astype(BF16)

    return [setup] + [functools.partial(block, r0) for r0 in range(0, t_rows, rows)]


def _in_proj_kernel(*refs, dils, windows):
    if windows is None:
        (x_ref, mod_ref, w_ref, b_ref, q0, q1, q2, kv0, kv1, kv2, u_ref, ga_ref, gb_ref, zs_ref) = refs
    else:
        (x_ref, mod_ref, w_ref, b_ref, wkv_ref, bkv_ref, cw_ref, cb_ref, cg_ref, cbeta_ref,
         q0, q1, q2, kv0, kv1, kv2, act_ref, ut_ref, ga_ref, gb_ref, t0, t1, t2,
         zs_ref, sh_ref, hist_ref) = refs
    x = x_ref[0]
    tm = x.shape[0]
    sh = mod_ref[0, :, 0:D_MODEL]
    sc = mod_ref[0, :, D_MODEL:2 * D_MODEL]
    h = (x * (1.0 + sc) + sh).astype(BF16)

    if windows is not None:
        i = pl.program_id(1)
        n_tiles = pl.num_programs(1)
        for g, t_ref in enumerate((t0, t1, t2)):
            @pl.when(i >= n_tiles - max(windows[g] // tm, 1))
            def _(g=g, t_ref=t_ref):
                bias = jnp.concatenate([bkv_ref[0, g]] * (tm // SPAD), axis=1)
                kvt = lax.dot_general(wkv_ref[0, g], h, _NT, preferred_element_type=F32) + bias
                t_ref[0] = kvt if windows[g] >= tm else kvt[:, tm - windows[g]:]

    def proj(c0, c1):
        return jnp.dot(h, w_ref[0, :, c0:c1], preferred_element_type=F32) + b_ref[0, :, c0:c1]

    def put(out_ref, col0, z, dil):
        if dil == 1:
            out_ref[0, :, col0:col0 + LANE_W] = z.astype(out_ref.dtype)
            return
        nt = LANE_W // VREG_LANES
        for c in range(nt):
            zs_ref[c] = z[:, c * VREG_LANES:(c + 1) * VREG_LANES]
        blk = out_ref.shape[2] // dil
        for r in range(dil):
            for c in range(nt):
                piece = zs_ref[c, pl.ds(r, tm // dil, stride=dil), :]
                lo = r * blk + col0 + c * VREG_LANES
                out_ref[0, :, lo:lo + VREG_LANES] = piece.astype(out_ref.dtype)

    o1 = 3 * ATT_W
    o2 = o1 + 2 * CONV_CH
    glu_a = proj(o1, o1 + CONV_CH)
    glu_b = proj(o1 + CONV_CH, o1 + 2 * CONV_CH)
    u = glu_a * _sigmoid(glu_b)
    conv_steps = []
    if windows is None:
        u_ref[0] = u
    else:
        prev = jnp.where(pl.program_id(1) == 0, 0.0, hist_ref[...])
        conv_steps = _conv_steps(sh_ref, prev, u, cw_ref, cb_ref[0], cg_ref[0], cbeta_ref[0], act_ref)
        conv_steps.pop(0)()
        hist_ref[...] = u[tm - CONV_HALO:tm]
        ut_ref[0] = u[tm - CONV_HALO:tm]

    jobs = []
    for g, (q_ref, kv_ref) in enumerate(((q0, kv0), (q1, kv1), (q2, kv2))):
        c = g * LANE_W
        jobs.append(functools.partial(
            lambda q_ref, c, g: put(q_ref, 0, proj(c, c + LANE_W) * (HEAD_DIM ** -0.5), dils[g]), q_ref, c, g))
        jobs.append(functools.partial(
            lambda kv_ref, c, g: put(kv_ref, 0, proj(ATT_W + c, ATT_W + c + LANE_W), dils[g]), kv_ref, c, g))
        jobs.append(functools.partial(
            lambda kv_ref, c, g: put(kv_ref, LANE_W, proj(2 * ATT_W + c, 2 * ATT_W + c + LANE_W), dils[g]),
            kv_ref, c, g))
    half = D_MODEL // 2
    for gate_ref, c0 in ((ga_ref, o2), (gb_ref, o2 + D_MODEL)):
        for hc in (0, half):
            def gate_job(gate_ref=gate_ref, c0=c0, hc=hc):
                gate_ref[0, :, hc:hc + half] = _sigmoid(proj(c0 + hc, c0 + hc + half)).astype(BF16)
            jobs.append(gate_job)
    per_job = -(-len(conv_steps) // len(jobs))
    for job in jobs:
        job()
        for _ in range(min(per_job, len(conv_steps))):
            conv_steps.pop(0)()


def in_proj(x, mod, w_bf, b, layer, tm, dils, wkv_t=None, bkv_t=None, windows=None, conv=None):
    g_, r, d = x.shape
    n_tiles = r // tm
    row = lambda w: pl.BlockSpec((1, tm, w), lambda g, i: (g, i, 0))
    fold = lambda w, dil: pl.BlockSpec((1, tm // dil, dil * w), lambda g, i: (g, i, 0))
    sds = lambda w, dt: jax.ShapeDtypeStruct((g_, r, w), dt)
    fsds = lambda w, dil, dt: jax.ShapeDtypeStruct((g_, r // dil, dil * w), dt)
    const = lambda a: pl.BlockSpec((1,) + a.shape[1:], lambda g, i: (layer,) + (0,) * (a.ndim - 1),
                                   pipeline_mode=pl.Buffered(1))
    in_specs = [row(d), _mod_spec(mod, tm, 2 * d, 0), const(w_bf), _layer_spec(b, layer)]
    args = [x, mod, w_bf, b]
    qkv_specs = [fold(LANE_W, dl) for dl in dils] + [fold(2 * LANE_W, dl) for dl in dils]
    qkv_shape = [fsds(LANE_W, dl, BF16) for dl in dils] + [fsds(2 * LANE_W, dl, F32) for dl in dils]
    scratch = [pltpu.VMEM((LANE_W // VREG_LANES, tm, VREG_LANES), F32)]
    if windows is None:
        out_specs = qkv_specs + [row(CONV_CH), row(d), row(d)]
        out_shape = qkv_shape + [sds(CONV_CH, F32), sds(d, BF16), sds(d, BF16)]
    else:
        in_specs += [const(wkv_t), _layer_spec(bkv_t, layer)] + [_layer_spec(a, layer) for a in conv]
        args += [wkv_t, bkv_t, *conv]
        out_specs = qkv_specs + [row(CONV_CH), pl.BlockSpec((1, CONV_HALO, CONV_CH), lambda g, i: (g, 0, 0)),
                                 row(d), row(d)]
        out_shape = qkv_shape + [sds(CONV_CH, BF16), jax.ShapeDtypeStruct((g_, CONV_HALO, CONV_CH), F32),
                                 sds(d, BF16), sds(d, BF16)]
        scratch += [pltpu.VMEM((VREG_SUBLANES, CONV_HALO + tm, CONV_CH), F32),
                    pltpu.VMEM((CONV_HALO, CONV_CH), F32)]
        for w in windows:
            assert w % tm == 0 or tm % w == 0
            first = n_tiles - max(w // tm, 1)
            out_specs.append(pl.BlockSpec((1, 2 * LANE_W, min(w, tm)),
                                          lambda g, i, first=first: (g, 0, jnp.maximum(i - first, 0))))
            out_shape.append(jax.ShapeDtypeStruct((g_, 2 * LANE_W, w), F32))
    return pl.pallas_call(
        functools.partial(_in_proj_kernel, dils=dils, windows=windows),
        grid=(g_, n_tiles),
        in_specs=in_specs,
        out_specs=out_specs,
        out_shape=out_shape,
        scratch_shapes=scratch,
        compiler_params=_cparams("arbitrary", "arbitrary"),
        name="in_proj",
    )(*args)


def _attn_kernel(q_ref, kvp_ref, kvc_ref, bias_ref, o_ref, l_ref, *, qb):
    i = pl.program_id(2)
    head_of_lane = lax.broadcasted_iota(I32, (QBLK, LANE_W), 1) >> 6
    ones = jnp.ones((2 * QBLK, VREG_LANES), BF16)
    first_cols = lax.broadcasted_iota(I32, (LANES * QBLK, 2 * QBLK), 1) < QBLK
    for j in range(qb):
        rs = slice(j * QBLK, (j + 1) * QBLK)
        q = q_ref[0, rs, :]
        q4 = jnp.concatenate([jnp.where(head_of_lane == h, q, jnp.zeros_like(q)) for h in range(LANES)], axis=0)
        if j == 0:
            kv = jnp.concatenate([kvp_ref[0], kvc_ref[0, rs, :]], axis=0)
        else:
            kv = kvc_ref[0, (j - 1) * QBLK:(j + 1) * QBLK, :]
        k2, v2 = kv[:, 0:LANE_W].astype(BF16), kv[:, LANE_W:2 * LANE_W].astype(BF16)
        s = lax.dot_general(q4, k2, _NT, preferred_element_type=F32) + bias_ref[...]
        if j == 0:
            s = jnp.where(jnp.logical_and(first_cols, i == 0), NEG_INF, s)
        m = jnp.max(s, axis=-1, keepdims=True)
        p = jnp.exp(s - m).astype(BF16)
        l = jnp.dot(p, ones, preferred_element_type=F32)
        o4 = jnp.dot(p, v2, preferred_element_type=F32)
        inv = 1.0 / l
        o4 = o4 * jnp.concatenate([inv, inv], axis=1)
        lse = m + jnp.log(l)
        lse2 = jnp.concatenate([lse, lse], axis=1)
        o, ls = o4[0:QBLK], lse2[0:QBLK]
        for h in range(1, LANES):
            sel = head_of_lane == h
            o = jnp.where(sel, o4[h * QBLK:(h + 1) * QBLK], o)
            ls = jnp.where(sel, lse2[h * QBLK:(h + 1) * QBLK], ls)
        o_ref[0, rs, :] = o.astype(BF16)
        l_ref[0, rs, :] = ls


def _band_bias(slopes_g, dil):
    qi = jnp.arange(QBLK)[:, None]
    kj = jnp.arange(QBLK)[None, :]
    dist_a = qi + BAND - kj
    dist_b = qi - kj
    sl = -(slopes_g * dil)[:, None, None]
    ba = jnp.where((dist_a <= BAND)[None], sl * dist_a.astype(F32)[None], NEG_INF)
    bb = jnp.where((dist_b >= 0)[None], sl * dist_b.astype(F32)[None], NEG_INF)
    return jnp.concatenate([ba, bb], axis=2).astype(F32).reshape(LANES * QBLK, 2 * QBLK)


def attn_prompt(qf, kvf, slopes_g, dil):
    b_, l_, _ = qf.shape
    qb = 4 if l_ % (4 * QBLK) == 0 else (2 if l_ % (2 * QBLK) == 0 else 1)
    nb = l_ // (qb * QBLK)
    bias = _band_bias(slopes_g, dil)
    bias_spec = pl.BlockSpec((LANES * QBLK, 2 * QBLK), lambda b, r, i: (0, 0))
    return pl.pallas_call(
        functools.partial(_attn_kernel, qb=qb),
        grid=(b_, dil, nb),
        in_specs=[pl.BlockSpec((1, qb * QBLK, LANE_W), lambda b, r, i: (b, i, r)),
                  pl.BlockSpec((1, QBLK, 2 * LANE_W), lambda b, r, i: (b, jnp.maximum(i * qb - 1, 0), r)),
                  pl.BlockSpec((1, qb * QBLK, 2 * LANE_W), lambda b, r, i: (b, i, r)),
                  bias_spec],
        out_specs=[pl.BlockSpec((1, qb * QBLK, LANE_W), lambda b, r, i: (b, i, r))] * 2,
        out_shape=[jax.ShapeDtypeStruct((b_, l_, dil * LANE_W), BF16),
                   jax.ShapeDtypeStruct((b_, l_, dil * LANE_W), F32)],
        compiler_params=_cparams("arbitrary", "arbitrary", "arbitrary"),
        name="attn_prompt",
    )(qf, kvf, kvf, bias)


SQ = 8
SPAD = 128


def _attn_s_kernel(x_ref, mod_ref, wq_ref, bq_ref, wkv_ref, bkv_ref, cache_ref, bias_ref, _tails_in,
                   o_ref, l_ref, tail_ref, hpad_ref, *, lw, t_new):
    nq = LANES * SQ
    blk = 2 * SQ

    @pl.when(pl.program_id(0) == 0)
    def _():
        hpad_ref[0:SPAD - blk, :] = jnp.zeros((SPAD - blk, D_MODEL), BF16)

    sh = mod_ref[0, :, 0:D_MODEL]
    sc = mod_ref[0, :, D_MODEL:2 * D_MODEL]
    h8 = x_ref[0] * (1.0 + sc) + sh
    h16 = jnp.concatenate([jnp.zeros_like(h8), h8], axis=0).astype(BF16)
    hpad_ref[SPAD - blk:SPAD, :] = h16
    q16 = (jnp.dot(h16, wq_ref[0], preferred_element_type=F32) + bq_ref[0]) * (HEAD_DIM ** -0.5)
    q8 = q16[SQ:blk].astype(BF16)
    head_of_lane = lax.broadcasted_iota(I32, (SQ, LANE_W), 1) >> 6
    q4 = jnp.concatenate([jnp.where(head_of_lane == h, q8, jnp.zeros_like(q8)) for h in range(LANES)], axis=0)
    kvn = lax.dot_general(wkv_ref[0, 0], hpad_ref[...], _NT, preferred_element_type=F32) + bkv_ref[0, 0]
    kc = cache_ref[0, 0, 0:LANE_W, :].astype(BF16)
    vc = cache_ref[0, 0, LANE_W:2 * LANE_W, :].astype(BF16)
    s_c = jnp.dot(q4, kc, preferred_element_type=F32) + bias_ref[:, 0:lw]
    s_n = jnp.dot(q4, kvn[0:LANE_W].astype(BF16), preferred_element_type=F32) + bias_ref[:, lw:lw + SPAD]
    m = jnp.maximum(jnp.max(s_c, axis=-1, keepdims=True), jnp.max(s_n, axis=-1, keepdims=True))
    p_c = jnp.exp(s_c - m)
    p_n = jnp.exp(s_n - m)
    l = jnp.sum(p_c, axis=-1, keepdims=True) + jnp.sum(p_n, axis=-1, keepdims=True)
    o4 = (lax.dot_general(p_c.astype(BF16), vc, _NT, preferred_element_type=F32)
          + lax.dot_general(p_n.astype(BF16), kvn[LANE_W:2 * LANE_W].astype(BF16), _NT,
                            preferred_element_type=F32)) / l
    lf = jnp.broadcast_to(m + jnp.log(l), (nq, LANE_W))
    rows = lax.broadcasted_iota(I32, (nq, LANE_W), 0)
    lanes = lax.broadcasted_iota(I32, (nq, LANE_W), 1)
    own = (rows >> 3) == (lanes >> 6)
    o4 = jnp.where(own, o4, 0.0)
    lf = jnp.where(own, lf, 0.0)
    o_ref[0] = (o4[0:SQ] + o4[SQ:2 * SQ] + o4[2 * SQ:3 * SQ] + o4[3 * SQ:4 * SQ]).astype(BF16)
    l_ref[0] = lf[0:SQ] + lf[SQ:2 * SQ] + lf[2 * SQ:3 * SQ] + lf[3 * SQ:4 * SQ]
    rolled = pltpu.roll(cache_ref[0, 0], lw - t_new, 1)
    if lw > SPAD:
        tail_ref[0, 0, :, 0:lw - SPAD] = rolled[:, 0:lw - SPAD]
    lane = lax.broadcasted_iota(I32, (2 * LANE_W, SPAD), 1)
    tail_ref[0, 0, :, lw - SPAD:lw] = jnp.where(lane >= SPAD - t_new, kvn, rolled[:, lw - SPAD:lw])


def _sample_bias(slopes_g, dil, lw, t_new):
    t = jnp.maximum(jnp.arange(SQ) - (SQ - t_new), 0)[:, None]
    r = jnp.arange(lw + SPAD)[None, :]
    pos = jnp.where(r < lw, r, r - (SPAD - t_new))
    dist = lw + t - pos
    jn = dist // dil
    real = (r < lw) | (r >= lw + SPAD - t_new)
    valid = real & (dist >= 0) & (dist % dil == 0) & (jn <= BAND)
    sl = (slopes_g * dil)[:, None, None]
    bias = jnp.where(valid[None], -sl * jn.astype(F32)[None], NEG_INF)
    return bias.reshape(LANES * SQ, lw + SPAD).astype(F32)


def attn_sample(x8, mod, w_bf, b3, wkv_t, bkv_t, cache_t, tails, layer, g, slopes_g, dil, t_new):
    db = x8.shape[0]
    depth, _, ch, lw = cache_t.shape
    bias = _sample_bias(slopes_g, dil, lw, t_new)
    in_specs = [pl.BlockSpec((1, SQ, D_MODEL), lambda b: (b, 0, 0)),
                pl.BlockSpec((1, 1, 2 * D_MODEL), lambda b: (b, 0, 0)),
                pl.BlockSpec((1, D_MODEL, LANE_W), lambda b: (layer, 0, g)),
                pl.BlockSpec((1, 1, LANE_W), lambda b: (layer, 0, g)),
                pl.BlockSpec((1, 1, ch, D_MODEL), lambda b: (layer, g, 0, 0)),
                pl.BlockSpec((1, 1, ch, SPAD), lambda b: (layer, g, 0, 0)),
                pl.BlockSpec((1, 1, ch, lw), lambda b: (layer, b, 0, 0)),
                pl.BlockSpec((LANES * SQ, lw + SPAD), lambda b: (0, 0)),
                pl.BlockSpec(memory_space=pl.ANY)]
    args = [x8, mod, w_bf, b3, wkv_t, bkv_t, cache_t, bias, tails]
    aliases = {len(args) - 1: 2}
    return pl.pallas_call(
        functools.partial(_attn_s_kernel, lw=lw, t_new=t_new),
        grid=(db,),
        in_specs=in_specs,
        out_specs=[pl.BlockSpec((1, SQ, LANE_W), lambda b: (b, 0, 0)),
                   pl.BlockSpec((1, SQ, LANE_W), lambda b: (b, 0, 0)),
                   pl.BlockSpec((1, 1, ch, lw), lambda b: (layer, b, 0, 0))],
        out_shape=[jax.ShapeDtypeStruct((db, SQ, LANE_W), BF16),
                   jax.ShapeDtypeStruct((db, SQ, LANE_W), F32),
                   jax.ShapeDtypeStruct((depth, db, ch, lw), F32)],
        scratch_shapes=[pltpu.VMEM((SPAD, D_MODEL), BF16)],
        input_output_aliases=aliases,
        compiler_params=_cparams("arbitrary"),
        name="attn_sample",
    )(*args)


CONV_ROWS = 16


def _conv_kernel(prev_ref, cur_ref, w_ref, cb_ref, g_ref, b_ref, o_ref, sh_ref, *, zero_first):
    prev = prev_ref[0]
    if zero_first:
        prev = jnp.where(pl.program_id(1) == 0, 0.0, prev)
    _conv_ln_swish(sh_ref, prev, cur_ref[0], w_ref, cb_ref[0], g_ref[0], b_ref[0], o_ref)


def conv_branch(u, prev_src, conv_w, conv_b, ln_g, ln_b, layer, tm, zero_first):
    g_, r, c = u.shape
    per = tm // CONV_HALO
    if zero_first:
        prev_map = lambda g, i: (g, jnp.maximum(i * per - 1, 0), 0)
    else:
        prev_map = lambda g, i: (g, 0, 0)
    return pl.pallas_call(
        functools.partial(_conv_kernel, zero_first=zero_first),
        grid=(g_, r // tm),
        in_specs=[pl.BlockSpec((1, CONV_HALO, c), prev_map),
                  pl.BlockSpec((1, tm, c), lambda g, i: (g, i, 0)),
                  _layer_spec(conv_w, layer), _layer_spec(conv_b, layer),
                  _layer_spec(ln_g, layer), _layer_spec(ln_b, layer)],
        out_specs=pl.BlockSpec((1, tm, c), lambda g, i: (g, i, 0)),
        out_shape=jax.ShapeDtypeStruct((g_, r, c), BF16),
        scratch_shapes=[pltpu.VMEM((VREG_SUBLANES, CONV_HALO + tm, c), F32)],
        compiler_params=_cparams("arbitrary", "arbitrary"),
        name="conv_branch",
    )(prev_src, u, conv_w, conv_b, ln_g, ln_b)


def _mix_kernel(o0, o1, o2, l0, l1, l2, cb_ref, ga_ref, gb_ref, x_ref, mod_ref,
                woa_ref, boa_ref, wpb_ref, bpb_ref, wout_ref, bout_ref, g1_ref, b1_ref,
                wrh_ref, wrl_ref, br_ref, tri_ref, cin_ref, x1_ref, hu_ref, route_ref, cnt_ref,
                s0, s1, s2, s3, carry_ref, *, dils):
    tm = x_ref.shape[1]

    @pl.when(jnp.logical_and(pl.program_id(0) == 0, pl.program_id(1) == 0))
    def _():
        carry_ref[...] = cin_ref[...]

    def unfold(ref, dil, scratch):
        if dil == 1:
            return ref[0].astype(F32)
        nt = LANE_W // VREG_LANES
        for r in range(dil):
            for c in range(nt):
                lo = r * LANE_W + c * VREG_LANES
                scratch[c, pl.ds(r, tm // dil, stride=dil), :] = ref[0, :, lo:lo + VREG_LANES].astype(F32)
        return jnp.concatenate([scratch[c] for c in range(nt)], axis=1)

    assert dils[0] == 1
    oa, ob, oc = unfold(o0, 1, None), unfold(o1, dils[1], s0), unfold(o2, dils[2], s1)
    la, lb, lc = unfold(l0, 1, None), unfold(l1, dils[1], s2), unfold(l2, dils[2], s3)
    lmax = jnp.maximum(jnp.maximum(la, lb), lc)
    ea, eb, ec = jnp.exp(la - lmax), jnp.exp(lb - lmax), jnp.exp(lc - lmax)
    merged = (ea * oa + eb * ob + ec * oc) / (ea + eb + ec)
    br_a = jnp.dot(merged.astype(BF16), woa_ref[0], preferred_element_type=F32) + boa_ref[0]
    br_b = jnp.dot(cb_ref[0], wpb_ref[0], preferred_element_type=F32) + bpb_ref[0]
    mixed = ga_ref[0].astype(F32) * br_a + gb_ref[0].astype(F32) * br_b
    y = jnp.dot(mixed.astype(BF16), wout_ref[0], preferred_element_type=F32) + bout_ref[0]
    gt1 = mod_ref[0, :, 2 * D_MODEL:3 * D_MODEL]
    sh2 = mod_ref[0, :, 3 * D_MODEL:4 * D_MODEL]
    sc2 = mod_ref[0, :, 4 * D_MODEL:5 * D_MODEL]
    x1 = _layer_norm(ALPHA_DN * x_ref[0] + (1.0 + gt1) * y, g1_ref[0], b1_ref[0])
    x1_ref[0] = x1
    h2 = x1 * (1.0 + sc2) + sh2
    hi = h2.astype(BF16)
    lo = (h2 - hi.astype(F32)).astype(BF16)
    logits = (lax.dot_general(wrh_ref[0], hi, _NT, preferred_element_type=F32)
              + lax.dot_general(wrh_ref[0], lo, _NT, preferred_element_type=F32)
              + lax.dot_general(wrl_ref[0], hi, _NT, preferred_element_type=F32)
              + jnp.concatenate([br_ref[0]] * (tm // VREG_LANES), axis=1))
    _store_row_words(hu_ref.at[0], _pack_bf16_pair(h2[:, 0:PACK_W], h2[:, PACK_W:D_MODEL]))
    _route_tile(logits, tri_ref, carry_ref, route_ref, cnt_ref)


def mix(o_g, l_g, cb, ga, gb, x, mod, wts, counts_in, layer, tm, dils):
    g_, r, d = x.shape
    row = lambda w: pl.BlockSpec((1, tm, w), lambda g, i: (g, i, 0))
    fold = lambda dil: pl.BlockSpec((1, tm // dil, dil * LANE_W), lambda g, i: (g, i, 0))
    whole = lambda a: pl.BlockSpec(a.shape, lambda g, i: (0,) * a.ndim)
    tri = (jnp.arange(tm)[:, None] < jnp.arange(tm)[None, :]).astype(BF16)
    return pl.pallas_call(
        functools.partial(_mix_kernel, dils=dils),
        grid=(g_, r // tm),
        in_specs=[fold(dl) for dl in dils] * 2 + [row(CONV_CH), row(d), row(d), row(d), _mod_spec(mod, tm, 6 * d, 0)]
                 + [_layer_spec(a, layer) for a in wts] + [whole(tri), whole(counts_in)],
        out_specs=[row(d), pl.BlockSpec((1, tm * ROW_SUB, ROW_LANE), lambda g, i: (g, i, 0)),
                   pl.BlockSpec((1, ROUTE_ROWS, tm), lambda g, i: (g, 0, i)),
                   pl.BlockSpec((N_EXPERTS, ROUTER_LANES), lambda g, i: (0, 0))],
        out_shape=[jax.ShapeDtypeStruct((g_, r, d), F32),
                   jax.ShapeDtypeStruct((g_, r * ROW_SUB, ROW_LANE), U32),
                   jax.ShapeDtypeStruct((g_, ROUTE_ROWS, r), F32),
                   jax.ShapeDtypeStruct((N_EXPERTS, ROUTER_LANES), F32)],
        scratch_shapes=[pltpu.VMEM((LANE_W // VREG_LANES, tm, VREG_LANES), F32)] * 4
                       + [pltpu.VMEM((N_EXPERTS, ROUTER_LANES), F32)],
        compiler_params=_cparams("arbitrary", "arbitrary"),
        name="mix",
    )(*o_g, *l_g, cb, ga, gb, x, mod, *wts, tri, counts_in)


ROUTE_ROWS = 16


def _route_tile(work, tri_ref, carry_ref, out_ref, cnt_ref):
    tm = work.shape[1]
    row = lax.broadcasted_iota(I32, (N_EXPERTS, tm), 0)
    vals, idxs, hots = [], [], []
    for _ in range(TOP_K):
        m = jnp.max(work, axis=0, keepdims=True)
        idx = jnp.min(jnp.where(work == m, row, N_EXPERTS), axis=0, keepdims=True)
        hot = row == idx
        work = jnp.where(hot, -jnp.inf, work)
        vals.append(m)
        idxs.append(idx)
        hots.append(hot)
    es = [jnp.exp(v - vals[0]) for v in vals]
    den = es[0] + es[1] + es[2] + es[3]
    member = jnp.where(hots[0] | hots[1] | hots[2] | hots[3], 1.0, 0.0)
    carry = carry_ref[...]
    before = (jnp.dot(member.astype(BF16), tri_ref[...], preferred_element_type=F32)
              + jnp.concatenate([carry] * (tm // VREG_LANES), axis=1))
    rec_row = lax.broadcasted_iota(I32, (ROUTE_ROWS, tm), 0)
    out = jnp.zeros((ROUTE_ROWS, tm), F32)
    for k in range(TOP_K):
        rank = jnp.sum(jnp.where(hots[k], before, 0.0), axis=0, keepdims=True)
        out = jnp.where(rec_row == k, es[k] / den, out)
        out = jnp.where(rec_row == TOP_K + k, idxs[k].astype(F32), out)
        out = jnp.where(rec_row == 2 * TOP_K + k, rank, out)
    out_ref[0] = out
    carry_ref[...] = carry + jnp.sum(member, axis=1, keepdims=True)
    cnt_ref[...] = carry_ref[...]


def _moe_kernel(be_ref, nu_ref, xs_ref, wup_ref, bup_ref, wdn_ref, bdn_ref, y_ref, wup_bf, wdn_bf, act_ref):
    i = pl.program_id(0)

    @pl.when(i < nu_ref[0])
    def _():
        prev_e = be_ref[jnp.maximum(i - 1, 0)]

        @pl.when((i == 0) | (be_ref[i] != prev_e))
        def _():
            step = 128
            for r0 in range(0, D_MODEL, step):
                wup_bf[r0:r0 + step, :] = wup_ref[0, 0, r0:r0 + step, :].astype(BF16)
            for r0 in range(0, D_FF, step):
                wdn_bf[r0:r0 + step, :] = wdn_ref[0, 0, r0:r0 + step, :].astype(BF16)

        lo, hi = _unpack_bf16_pair(_load_row_words(xs_ref))
        x = jnp.concatenate([lo, hi], axis=1).astype(BF16)
        cw = 512
        for c0 in range(0, D_FF, cw):
            a = jnp.dot(x, wup_bf[:, c0:c0 + cw], preferred_element_type=F32) + bup_ref[0, 0, :, c0:c0 + cw]
            lin = (jnp.dot(x, wup_bf[:, D_FF + c0:D_FF + c0 + cw], preferred_element_type=F32)
                   + bup_ref[0, 0, :, D_FF + c0:D_FF + c0 + cw])
            a = jnp.minimum(a, SWIGLU_LIMIT)
            lin = jnp.clip(lin, -SWIGLU_LIMIT, SWIGLU_LIMIT)
            act_ref[:, c0:c0 + cw] = (a * _sigmoid(SWIGLU_ALPHA * a) * (lin + 1.0)).astype(BF16)
        y = jnp.dot(act_ref[...], wdn_bf[...], preferred_element_type=F32) + bdn_ref[0, 0]
        _store_row_words(y_ref, _pack_bf16_pair_native(y[:, 0:PACK_W], y[:, PACK_W:D_MODEL]))


def moe_experts(xs, block_e, n_used, w_up, b_up, w_down, b_down, layer):
    cap = xs.shape[0] // ROW_SUB
    nblk = cap // MOE_TM
    row_map = lambda i, be, nu: (jnp.minimum(i, nu[0] - 1), 0)
    exp_map = lambda i, be, nu: (layer, be[i], 0, 0)
    grid_spec = pltpu.PrefetchScalarGridSpec(
        num_scalar_prefetch=2,
        grid=(nblk,),
        in_specs=[pl.BlockSpec((MOE_TM * ROW_SUB, ROW_LANE), row_map),
                  pl.BlockSpec((1, 1, D_MODEL, 2 * D_FF), exp_map),
                  pl.BlockSpec((1, 1, 1, 2 * D_FF), exp_map),
                  pl.BlockSpec((1, 1, D_FF, D_MODEL), exp_map),
                  pl.BlockSpec((1, 1, 1, D_MODEL), exp_map)],
        out_specs=pl.BlockSpec((MOE_TM * ROW_SUB, ROW_LANE), row_map),
        scratch_shapes=[pltpu.VMEM((D_MODEL, 2 * D_FF), BF16),
                        pltpu.VMEM((D_FF, D_MODEL), BF16),
                        pltpu.VMEM((MOE_TM, D_FF), BF16)],
    )
    return pl.pallas_call(
        _moe_kernel,
        grid_spec=grid_spec,
        out_shape=jax.ShapeDtypeStruct((cap * ROW_SUB, ROW_LANE), U32),
        compiler_params=_cparams("arbitrary"),
        name="moe_experts",
    )(block_e, n_used, xs, w_up, b_up, w_down, b_down)


def _final_kernel(y0, y1, y2, y3, r_ref, x_ref, mod_ref, g2_ref, b2_ref, o_ref):
    gates = r_ref[...]
    acc_lo = None
    for k, y_ref in enumerate((y0, y1, y2, y3)):
        lo, hi = _unpack_bf16_pair(_load_row_words(y_ref))
        gk = gates[:, k:k + 1]
        acc_lo = gk * lo if acc_lo is None else acc_lo + gk * lo
        acc_hi = gk * hi if k == 0 else acc_hi + gk * hi
    y = jnp.concatenate([acc_lo, acc_hi], axis=1)
    gt2 = mod_ref[0]
    o_ref[0] = _layer_norm(ALPHA_DN * x_ref[0] + (1.0 + gt2) * y, g2_ref[0], b2_ref[0])


def final(yg, route, x1, mod, ln_g, ln_b, layer, tm, row0):
    g_, r, d = x1.shape
    per = r // tm
    base = row0 // tm
    nblk_k = yg.shape[0] // (TOP_K * tm * ROW_SUB)
    yspec = lambda k: pl.BlockSpec((tm * ROW_SUB, ROW_LANE), lambda g, i: (k * nblk_k + base + g * per + i, 0))
    return pl.pallas_call(
        _final_kernel,
        grid=(g_, per),
        in_specs=[yspec(0), yspec(1), yspec(2), yspec(3),
                  pl.BlockSpec((tm, ROUTER_LANES), lambda g, i: (g * per + i, 0)),
                  pl.BlockSpec((1, tm, d), lambda g, i: (g, i, 0)), _mod_spec(mod, tm, d, 5),
                  _layer_spec(ln_g, layer), _layer_spec(ln_b, layer)],
        out_specs=pl.BlockSpec((1, tm, d), lambda g, i: (g, i, 0)),
        out_shape=jax.ShapeDtypeStruct((g_, r, d), F32),
        compiler_params=_cparams("arbitrary", "arbitrary"),
        name="final",
    )(yg, yg, yg, yg, route, x1, mod, ln_g, ln_b)


SC_CORES = 2
SC_SUBCORES = 16
SC_WORKERS = SC_CORES * SC_SUBCORES
SC_CHUNK = 128


def _sc_mesh():
    return plsc.VectorSubcoreMesh(core_axis_name="c", subcore_axis_name="s")


def _sc_worker_chunks(nchunk):
    wid = lax.axis_index("s") * SC_CORES + lax.axis_index("c")
    return wid, (nchunk - wid + SC_WORKERS - 1) // SC_WORKERS


def dispatch_rows(hu_a, hu_b, dest4, cap):
    chunks_a = hu_a.shape[0] // SC_CHUNK
    nchunk = chunks_a + hu_b.shape[0] // SC_CHUNK
    idx = dest4.reshape(TOP_K, nchunk, SC_CHUNK)

    @functools.partial(
        pl.kernel, mesh=_sc_mesh(),
        out_type=jax.ShapeDtypeStruct((cap, ROW_SUB, ROW_LANE), U32),
        scratch_types=[pltpu.VMEM((SC_CHUNK,), I32)] * TOP_K
                      + [pltpu.VMEM((SC_CHUNK, ROW_SUB, ROW_LANE), U32)])
    def k(a_hbm, b_hbm, i_hbm, o_hbm, i0, i1, i2, i3, rows_v):
        wid, n_mine = _sc_worker_chunks(nchunk)

        @pl.loop(0, n_mine)
        def _(j):
            c = wid + j * SC_WORKERS

            @pl.when(c < chunks_a)
            def _():
                pltpu.sync_copy(a_hbm.at[pl.ds(c * SC_CHUNK, SC_CHUNK)], rows_v)

            @pl.when(c >= chunks_a)
            def _():
                pltpu.sync_copy(b_hbm.at[pl.ds((c - chunks_a) * SC_CHUNK, SC_CHUNK)], rows_v)

            for kk, iv in enumerate((i0, i1, i2, i3)):
                pltpu.sync_copy(i_hbm.at[kk, c], iv)
            for iv in (i0, i1, i2, i3):
                pltpu.sync_copy(rows_v, o_hbm.at[iv])

    return k(hu_a, hu_b, idx)


def gather_rows(y, dest4):
    m = dest4.shape[0] * dest4.shape[1]
    nchunk = m // SC_CHUNK
    idx = dest4.reshape(nchunk, SC_CHUNK)

    @functools.partial(
        pl.kernel, mesh=_sc_mesh(),
        out_type=jax.ShapeDtypeStruct((m, ROW_SUB, ROW_LANE), U32),
        scratch_types=[pltpu.VMEM((SC_CHUNK,), I32), pltpu.VMEM((SC_CHUNK, ROW_SUB, ROW_LANE), U32)])
    def k(y_hbm, i_hbm, o_hbm, idx_v, rows_v):
        wid, n_mine = _sc_worker_chunks(nchunk)

        @pl.loop(0, n_mine)
        def _(j):
            c = wid + j * SC_WORKERS
            pltpu.sync_copy(i_hbm.at[c], idx_v)
            pltpu.sync_copy(y_hbm.at[idx_v], rows_v)
            pltpu.sync_copy(rows_v, o_hbm.at[pl.ds(c * SC_CHUNK, SC_CHUNK)])

    return k(y, idx)


def routing_plan(rec, counts, n_tok):
    top_i = rec[TOP_K:2 * TOP_K].astype(I32)
    rank = rec[2 * TOP_K:3 * TOP_K].astype(I32)
    cnt = counts[:, 0].astype(I32)
    padded = ((cnt + MOE_TM - 1) // MOE_TM) * MOE_TM
    pend = jnp.cumsum(padded)
    pstart = pend - padded
    hot = top_i[:, :, None] == jnp.arange(N_EXPERTS, dtype=I32)[None, None, :]
    dest = rank + jnp.sum(jnp.where(hot, pstart[None, None, :], 0), axis=-1)
    nblk = -(-(n_tok * TOP_K) // MOE_TM) + N_EXPERTS
    n_used = pend[-1] // MOE_TM
    blk = jnp.arange(nblk, dtype=I32)
    blk_c = jnp.minimum(blk, n_used - 1)
    block_e = jnp.sum((blk_c[:, None] * MOE_TM >= pend[None, :]).astype(I32), axis=1)
    block_e = jnp.minimum(block_e, N_EXPERTS - 1)
    return dest, block_e, n_used.reshape(1).astype(I32), nblk * MOE_TM


def _alibi_slopes():
    return jnp.exp2(-8.0 * (jnp.arange(N_HEADS, dtype=F32) + 1.0) / N_HEADS)


def _row3(a):
    return a.reshape(a.shape[0], 1, a.shape[1])


def kernel(x_prompt, x_sample, cache_win0, cache_win1, cache_win2, state_conv, c_prompt, c_sample,
           w_mod, b_mod, w_in, b_in, w_oa, b_oa, conv_w, conv_b, conv_ln_g, conv_ln_b, w_pb, b_pb,
           w_out, b_out, ln1_g, ln1_b, w_router, b_router, w_up, b_up, w_down, b_down, ln2_g, ln2_b):
    depth = w_in.shape[0]
    b_, s, d = x_prompt.shape
    db, t_new, _ = x_sample.shape
    n_p, n_s = b_ * s, db * t_new
    n_tok = n_p + n_s
    tm_p = 512
    no_fold = (1,) * N_GROUPS
    slopes = _alibi_slopes()
    caches = [jnp.transpose(c, (0, 1, 3, 4, 5, 2)).reshape(c.shape[0], c.shape[1], 2 * LANE_W, c.shape[2])
              for c in (cache_win0, cache_win1, cache_win2)]
    wkv = w_in[:, :, ATT_W:3 * ATT_W].reshape(depth, D_MODEL, 2, N_GROUPS, LANE_W)
    wkv_t = jnp.transpose(wkv, (0, 3, 2, 4, 1)).reshape(depth, N_GROUPS, 2 * LANE_W, D_MODEL).astype(BF16)
    bkv = jnp.transpose(b_in[:, ATT_W:3 * ATT_W].reshape(depth, 2, N_GROUPS, LANE_W), (0, 2, 1, 3))
    bkv_t = jnp.broadcast_to(bkv.reshape(depth, N_GROUPS, 2 * LANE_W, 1), (depth, N_GROUPS, 2 * LANE_W, SPAD))

    mod_all = adaln_mod(jnp.concatenate([c_prompt, c_sample], axis=0), w_mod, b_mod)
    w_in_bf = w_in.astype(BF16)
    b_in3 = _row3(b_in)
    wr_t = jnp.transpose(w_router, (0, 2, 1))
    wr_hi = wr_t.astype(BF16)
    wr_lo = (wr_t - wr_hi.astype(F32)).astype(BF16)
    br_t = jnp.broadcast_to(b_router[:, :, None], (depth, N_EXPERTS, ROUTER_LANES))
    mix_wts = (w_oa.astype(BF16), _row3(b_oa), w_pb.astype(BF16), _row3(b_pb), w_out.astype(BF16), _row3(b_out),
               _row3(ln1_g), _row3(ln1_b), wr_hi, wr_lo, br_t)
    conv_b3, cg3, cb3 = _row3(conv_b), _row3(conv_ln_g), _row3(conv_ln_b)
    conv_w = jnp.repeat(conv_w, VREG_SUBLANES, axis=1)
    ln2g3, ln2b3 = _row3(ln2_g), _row3(ln2_b)
    e = w_up.shape[1]
    b_up4 = b_up.reshape(depth, e, 1, 2 * D_FF)
    b_down4 = b_down.reshape(depth, e, 1, D_MODEL)

    xp = x_prompt
    xs_ = x_sample.reshape(1, n_s, d)
    wp = [[] for _ in range(N_GROUPS)]
    ws = [jnp.zeros(c.shape, F32) for c in caches]
    cp, cs = [], []
    for l in range(depth):
        mod_p = mod_all[l, :b_].reshape(b_, 1, 6 * d)
        mod_s = jnp.repeat(mod_all[l, b_:], t_new, axis=0).reshape(1, n_s, 6 * d)

        (pq0, pq1, pq2, pkv0, pkv1, pkv2, pcb, put, pga, pgb, pt0, pt1, pt2) = in_proj(
            xp, mod_p, w_in_bf, b_in3, l, tm_p, DILATIONS, wkv_t, bkv_t, WINDOWS,
            (conv_w, conv_b3, cg3, cb3))
        pkv = (pkv0, pkv1, pkv2)
        po, pl_ = [], []
        for g, qg in enumerate((pq0, pq1, pq2)):
            o, lse = attn_prompt(qg, pkv[g], slopes[g * LANES:(g + 1) * LANES], DILATIONS[g])
            po.append(o)
            pl_.append(lse)
            wp[g].append((pt0, pt1, pt2)[g])
        cp.append(put[:, CONV_HALO - (CONV_WIDTH - 1):])

        sq0, sq1, sq2, skv0, skv1, skv2, su, sga, sgb = in_proj(xs_, mod_s, w_in_bf, b_in3, l, n_s, no_fold)
        del sq0, sq1, sq2, skv0, skv1, skv2
        x8 = jnp.pad(xs_.reshape(db, t_new, d), ((0, 0), (SQ - t_new, 0), (0, 0)))
        mod_seq = mod_all[l, b_:].reshape(db, 1, 6 * d)
        so, sl_ = [], []
        for g in range(N_GROUPS):
            o, lse, ws[g] = attn_sample(x8, mod_seq, w_in_bf, b_in3, wkv_t, bkv_t, caches[g], ws[g], l, g,
                                        slopes[g * LANES:(g + 1) * LANES], DILATIONS[g], t_new)
            so.append(o[:, SQ - t_new:].reshape(1, n_s, LANE_W))
            sl_.append(lse[:, SQ - t_new:].reshape(1, n_s, LANE_W))
        su3 = su.reshape(db, t_new, CONV_CH)
        full_s = jnp.concatenate([state_conv[l], su3], axis=1)
        cs.append(full_s[:, t_new:])
        st_pad = jnp.pad(state_conv[l], ((0, 0), (CONV_HALO - (CONV_WIDTH - 1), 0), (0, 0)))
        su8 = jnp.pad(su3, ((0, 0), (0, SQ - t_new), (0, 0)))
        scb = conv_branch(su8, st_pad, conv_w, conv_b3, cg3, cb3, l, SQ, False)
        scb = scb[:, :t_new].reshape(1, n_s, CONV_CH)

        px1, phu, rec_p, counts_p = mix(po, pl_, pcb, pga, pgb, xp, mod_p, mix_wts,
                                        jnp.zeros((N_EXPERTS, ROUTER_LANES), F32), l, tm_p, DILATIONS)
        sx1, shu, rec_s, counts = mix(so, sl_, scb, sga, sgb, xs_, mod_s, mix_wts, counts_p, l, n_s, no_fold)
        rec = jnp.concatenate([jnp.transpose(rec_p, (1, 0, 2)).reshape(ROUTE_ROWS, n_p), rec_s[0]], axis=1)
        gate_rows = jnp.pad(rec[0:TOP_K].T, ((0, 0), (0, ROUTER_LANES - TOP_K)))
        route_p, route_s = gate_rows[:n_p], gate_rows[n_p:]

        dest4, block_e, n_used, cap = routing_plan(rec, counts, n_tok)
        xsort = dispatch_rows(phu.reshape(n_p, ROW_SUB, ROW_LANE), shu.reshape(n_s, ROW_SUB, ROW_LANE), dest4, cap)
        ysort = moe_experts(xsort.reshape(cap * ROW_SUB, ROW_LANE), block_e, n_used,
                            w_up, b_up4, w_down, b_down4, l)
        n_pad = -(-n_tok // tm_p) * tm_p
        filler = (jnp.arange(TOP_K * (n_pad - n_tok), dtype=I32) * 997) % cap
        dest_pad = jnp.concatenate([dest4, filler.reshape(TOP_K, n_pad - n_tok)], axis=1)
        yg = gather_rows(ysort.reshape(cap, ROW_SUB, ROW_LANE), dest_pad)
        yg = yg.reshape(TOP_K * n_pad * ROW_SUB, ROW_LANE)

        xp = final(yg, route_p, px1, mod_p, ln2g3, ln2b3, l, tm_p, 0)
        xs_ = final(yg, route_s, sx1, mod_s, ln2g3, ln2b3, l, n_s, n_p)

    stack = lambda xs: jnp.stack(xs)
    to_win = lambda w: jnp.transpose(w.reshape(depth, w.shape[1], 2, LANES, HEAD_DIM, w.shape[-1]), (0, 1, 5, 2, 3, 4))
    return (xp, xs_.reshape(db, t_new, d),
            to_win(stack(wp[0])), to_win(stack(wp[1])), to_win(stack(wp[2])), stack(cp),
            to_win(ws[0]), to_win(ws[1]), to_win(ws[2]), stack(cs))
```
